```python
import jax
import jax.numpy as jnp
from jax import lax
import numpy as np

D_MODEL = 1024
BATCH = 8
SEQ = 4096
DEPTH = 2
DEC_BATCH = 32
DEC_SEQ = 16
PAST_LEN = 4096

CHUNK = 64
Q_BLOCK = 128
N_HEADS = 8
QK_NOPE = 64
QK_ROPE = 32
QK_HEAD = QK_NOPE + QK_ROPE
V_HEAD = 64
Q_LORA = 384
KV_LORA = 256
ATTN_WIDTH = N_HEADS * V_HEAD
POOL_WINDOWS = (2, 4, 8, 16)
N_POOL_GROUPS = 4
POOL_GROUP = 128
POOL_WIDTH = N_POOL_GROUPS * POOL_GROUP
POOL_HIST = max(POOL_WINDOWS) - 1
D_FF = 4 * D_MODEL
ROPE_THETA = 10000.0
EPS = 1e-6
SM_SCALE = QK_HEAD ** -0.5

OFF_Q = 0
OFF_KV = OFF_Q + Q_LORA
OFF_KR = OFF_KV + KV_LORA
OFF_P = OFF_KR + QK_ROPE
OFF_GA = OFF_P + POOL_WIDTH
OFF_GB = OFF_GA + D_MODEL
IN_WIDTH = OFF_GB + D_MODEL

kernel_name = 'hybrid_mla_pool_stream_step'


def rmsnorm(x, g):
    xf = x.astype(jnp.float32)
    y = xf * lax.rsqrt(jnp.mean(xf * xf, axis=-1, keepdims=True) + EPS)
    return (y * g.astype(jnp.float32)).astype(x.dtype)


def rope(x, pos):
    half = QK_ROPE // 2
    inv = jnp.power(ROPE_THETA, -jnp.arange(half, dtype=jnp.float32) / half)
    ang = pos[:, None] * inv[None, :]
    shape = (1, ang.shape[0]) + (1,) * (x.ndim - 3) + (half,)
    cos = jnp.cos(ang).reshape(shape)
    sin = jnp.sin(ang).reshape(shape)
    xf = x.astype(jnp.float32)
    x1, x2 = xf[..., :half], xf[..., half:]
    return jnp.concatenate([x1 * cos - x2 * sin, x1 * sin + x2 * cos], axis=-1).astype(x.dtype)


def mla_keys_values(c, kr, w_ukv, g_k):
    kv = jnp.einsum('btc,chd->bthd', c, w_ukv)
    k_nope, v = kv[..., :QK_NOPE], kv[..., QK_NOPE:]
    kr_h = jnp.broadcast_to(kr[:, :, None, :], k_nope.shape[:3] + (QK_ROPE,)).astype(k_nope.dtype)
    k = rmsnorm(jnp.concatenate([k_nope, kr_h], axis=-1), g_k)
    return k, v


def mla_queries(cq, w_uq, g_q, pos):
    q = jnp.einsum('btc,chd->bthd', cq, w_uq)
    q = jnp.concatenate([q[..., :QK_NOPE], rope(q[..., QK_NOPE:], pos)], axis=-1)
    return rmsnorm(q, g_q) * SM_SCALE


def attend_prompt(q, k, v):
    B, S = q.shape[0], q.shape[1]
    nb = S // Q_BLOCK
    qb = q.reshape(B, nb, Q_BLOCK, N_HEADS, QK_HEAD).transpose(1, 0, 2, 3, 4)
    k_chunk = jnp.arange(S) // CHUNK

    def block(args):
        i, qi = args
        q_chunk = (i * Q_BLOCK + jnp.arange(Q_BLOCK)) // CHUNK
        mask = k_chunk[None, :] <= q_chunk[:, None]
        s = jnp.einsum('bqhd,bkhd->bhqk', qi, k).astype(jnp.float32)
        s = jnp.where(mask[None, None], s, -jnp.inf)
        p = jax.nn.softmax(s, axis=-1).astype(v.dtype)
        return jnp.einsum('bhqk,bkhd->bqhd', p, v)

    out = lax.map(block, (jnp.arange(nb), qb))
    return out.transpose(1, 0, 2, 3, 4).reshape(B, S, ATTN_WIDTH)


def attend_all(q, k, v):
    B, T = q.shape[0], q.shape[1]
    s = jnp.einsum('bqhd,bkhd->bhqk', q, k).astype(jnp.float32)
    p = jax.nn.softmax(s, axis=-1).astype(v.dtype)
    return jnp.einsum('bhqk,bkhd->bqhd', p, v).reshape(B, T, ATTN_WIDTH)


def pool_mix(p_ext, n_hist, pos0):
    B, L, W = p_ext.shape
    T = L - n_hist
    pf = p_ext.astype(jnp.float32)
    cs = jnp.concatenate([jnp.zeros((B, 1, W), jnp.float32), jnp.cumsum(pf, axis=1)], axis=1)
    t = jnp.arange(n_hist, L)
    pos = pos0 + jnp.arange(T)
    hi = cs[:, t + 1]
    outs = []
    for g, w in enumerate(POOL_WINDOWS):
        sl = slice(g * POOL_GROUP, (g + 1) * POOL_GROUP)
        lo_idx = jnp.maximum(t + 1 - w, 0)
        cnt = jnp.minimum(w, pos + 1).astype(jnp.float32)
        mean = (hi[..., sl] - cs[:, lo_idx, sl]) / cnt[None, :, None]
        outs.append(mean - pf[:, n_hist:, sl])
    return jnp.concatenate(outs, axis=-1).astype(p_ext.dtype)


def trunk_layer(x, pos0, hist_c, hist_kr, hist_p, g_mix, w_in, g_qa, g_kva, w_uq, w_ukv,
                g_q, g_k, w_attn_out, w_pool, pool_scale, w_pool_out, w_o, g_mlp, w_up, w_down):
    B, T, _ = x.shape
    pos = pos0 + jnp.arange(T, dtype=jnp.float32)
    h = rmsnorm(x, g_mix)
    z = jnp.einsum('btd,de->bte', h, w_in)
    cq = rmsnorm(z[..., OFF_Q:OFF_KV], g_qa)
    ckv = rmsnorm(z[..., OFF_KV:OFF_KR], g_kva)
    kr = rope(z[..., OFF_KR:OFF_P], pos)
    p = z[..., OFF_P:OFF_GA]
    gate_a = jax.nn.sigmoid(z[..., OFF_GA:OFF_GB])
    gate_b = jax.nn.sigmoid(z[..., OFF_GB:IN_WIDTH])
    q = mla_queries(cq, w_uq, g_q, pos)
    if hist_c is None:
        k, v = mla_keys_values(ckv, kr, w_ukv, g_k)
        attn = attend_prompt(q, k, v)
        p_ext, n_hist = p, 0
    else:
        c_all = jnp.concatenate([hist_c.astype(ckv.dtype), ckv], axis=1)
        kr_all = jnp.concatenate([hist_kr.astype(kr.dtype), kr], axis=1)
        k, v = mla_keys_values(c_all, kr_all, w_ukv, g_k)
        attn = attend_all(q, k, v)
        p_ext, n_hist = jnp.concatenate([hist_p.astype(p.dtype), p], axis=1), POOL_HIST
    pooled = pool_mix(p_ext, n_hist, pos0)
    u = jnp.einsum('btgc,gce->btge', pooled.reshape(B, T, N_POOL_GROUPS, POOL_GROUP), w_pool)
    u = u.reshape(B, T, POOL_WIDTH) * pool_scale
    branch_a = jnp.einsum('btc,cd->btd', attn, w_attn_out)
    branch_b = jnp.einsum('btc,cd->btd', u, w_pool_out)
    x = x + jnp.einsum('btd,de->bte', gate_a * branch_a + gate_b * branch_b, w_o)
    hm = rmsnorm(x, g_mlp)
    a = jnp.square(jax.nn.relu(jnp.einsum('btd,df->btf', hm, w_up)))
    x = x + jnp.einsum('btf,fd->btd', a, w_down)
    return x, ckv, kr, p_ext[:, -POOL_HIST:]


def setup_inputs(seed: int = 0) -> dict:
    key = jax.random.key(seed)
    ks = jax.random.split(key, 24)

    def nrm(k, shape, scale):
        return jax.random.normal(k, shape, jnp.float32) * scale

    def gain(k, shape):
        return 1.0 + 0.05 * jax.random.normal(k, shape, jnp.float32)

    return {
        'x_prompt': nrm(ks[0], (BATCH, SEQ, D_MODEL), 1.0),
        'x_sample': nrm(ks[1], (DEC_BATCH, DEC_SEQ, D_MODEL), 1.0),
        'cache_ckv': nrm(ks[2], (DEPTH, DEC_BATCH, PAST_LEN, KV_LORA), 1.0),
        'cache_krope': nrm(ks[3], (DEPTH, DEC_BATCH, PAST_LEN, QK_ROPE), 1.0),
        'state_pool': nrm(ks[4], (DEPTH, DEC_BATCH, POOL_HIST, POOL_WIDTH), 1.0),
        'g_mix': gain(ks[5], (DEPTH, D_MODEL)),
        'w_in': nrm(ks[6], (DEPTH, D_MODEL, IN_WIDTH), D_MODEL ** -0.5),
        'g_qa': gain(ks[7], (DEPTH, Q_LORA)),
        'g_kva': gain(ks[8], (DEPTH, KV_LORA)),
        'w_uq': nrm(ks[9], (DEPTH, Q_LORA, N_HEADS, QK_HEAD), Q_LORA ** -0.5),
        'w_ukv': nrm(ks[10], (DEPTH, KV_LORA, N_HEADS, QK_NOPE + V_HEAD), KV_LORA ** -0.5),
        'g_q': gain(ks[11], (DEPTH, QK_HEAD)),
        'g_k': gain(ks[12], (DEPTH, QK_HEAD)),
        'w_attn_out': nrm(ks[13], (DEPTH, ATTN_WIDTH, D_MODEL), ATTN_WIDTH ** -0.5),
        'w_pool': nrm(ks[14], (DEPTH, N_POOL_GROUPS, POOL_GROUP, POOL_GROUP), POOL_GROUP ** -0.5),
        'pool_scale': gain(ks[15], (DEPTH, POOL_WIDTH)),
        'w_pool_out': nrm(ks[16], (DEPTH, POOL_WIDTH, D_MODEL), POOL_WIDTH ** -0.5),
        'w_o': nrm(ks[17], (DEPTH, D_MODEL, D_MODEL), D_MODEL ** -0.5),
        'g_mlp': gain(ks[18], (DEPTH, D_MODEL)),
        'w_up': nrm(ks[19], (DEPTH, D_MODEL, D_FF), D_MODEL ** -0.5),
        'w_down': nrm(ks[20], (DEPTH, D_FF, D_MODEL), D_FF ** -0.5),
    }


def reference(x_prompt, x_sample, cache_ckv, cache_krope, state_pool, g_mix, w_in, g_qa, g_kva,
              w_uq, w_ukv, g_q, g_k, w_attn_out, w_pool, pool_scale, w_pool_out, w_o, g_mlp,
              w_up, w_down):
    past = cache_ckv.shape[2]
    yp, ys = x_prompt, x_sample
    ckv_p, kr_p, pool_p, ckv_s, kr_s, pool_s = [], [], [], [], [], []
    for l in range(DEPTH):
        w = (g_mix[l], w_in[l], g_qa[l], g_kva[l], w_uq[l], w_ukv[l], g_q[l], g_k[l],
             w_attn_out[l], w_pool[l], pool_scale[l], w_pool_out[l], w_o[l], g_mlp[l],
             w_up[l], w_down[l])
        yp, c1, r1, s1 = trunk_layer(yp, 0, None, None, None, *w)
        ys, c2, r2, s2 = trunk_layer(ys, past, cache_ckv[l], cache_krope[l], state_pool[l], *w)
        ckv_p.append(c1)
        kr_p.append(r1)
        pool_p.append(s1)
        ckv_s.append(c2)
        kr_s.append(r2)
        pool_s.append(s2)
    return (yp, ys, jnp.stack(ckv_p), jnp.stack(kr_p), jnp.stack(pool_p),
            jnp.stack(ckv_s), jnp.stack(kr_s), jnp.stack(pool_s))
```

```python
import functools

import jax
import jax.numpy as jnp
from jax import lax
from jax.experimental import pallas as pl
from jax.experimental.pallas import tpu as pltpu

F32 = jnp.float32
BF16 = jnp.bfloat16

LANE = 128
CHUNK = 64
N_HEADS = 8
QK_NOPE = 64
QK_ROPE = 32
QK_HEAD = QK_NOPE + QK_ROPE
V_HEAD = 64
Q_LORA = 384
KV_LORA = 256
ATTN_WIDTH = N_HEADS * V_HEAD
POOL_WINDOWS = (2, 4, 8, 16)
POOL_GROUP = 128
POOL_WIDTH = len(POOL_WINDOWS) * POOL_GROUP
POOL_HIST = max(POOL_WINDOWS) - 1
POOL_PAD = POOL_HIST + 1
ROPE_THETA = 10000.0
EPS = 1e-6
SM_SCALE = QK_HEAD ** -0.5
SLAB = N_HEADS * LANE
NEG = -1e30

PROJ_TM = 512
ATT_TQ = 256
ATT_TK = 256
MERGE_TM = 256
CACHE_TK = 512
VMEM_LIMIT = 56 * 1024 * 1024


def _const_spec(shape):
    nd = len(shape)
    return pl.BlockSpec(shape, lambda *_: (0,) * nd, pipeline_mode=pl.Buffered(1))


def _rms(x, g):
    ms = jnp.mean(x * x, axis=-1, keepdims=True)
    return x * lax.rsqrt(ms + EPS) * g


def _rope_slab(x, c, s1, s2):
    return x * c + pltpu.roll(x, 16, 1) * s1 + pltpu.roll(x, LANE - 16, 1) * s2


def _head_norm(x, g):
    ss = jnp.sum(x * x, axis=-1, keepdims=True)
    return x * lax.rsqrt(ss * (1.0 / QK_HEAD) + EPS) * g


def _proj_kernel(x_ref, c_ref, s1_ref, s2_ref, gmix_ref, w1_ref, gqa_ref, gkva_ref, wq_ref, wk_ref,
                 wv_ref, gq_ref, gk_ref, q_ref, k_ref, v_ref, ckv_ref, kr_ref):
    x = x_ref[...]
    h = _rms(x, gmix_ref[...]).astype(BF16)
    z = jnp.dot(h, w1_ref[...], preferred_element_type=F32)
    cq = _rms(z[:, :Q_LORA], gqa_ref[...]).astype(BF16)
    ckv = _rms(z[:, Q_LORA:Q_LORA + KV_LORA], gkva_ref[...])
    ckv_ref[...] = ckv
    ckv_b = ckv.astype(BF16)
    c, s1, s2 = c_ref[...], s1_ref[...], s2_ref[...]
    kr = _rope_slab(z[:, Q_LORA + KV_LORA:], c, s1, s2)
    kr_ref[...] = kr[:, :QK_ROPE]
    q = jnp.dot(cq, wq_ref[...], preferred_element_type=F32)
    kn = jnp.dot(ckv_b, wk_ref[...], preferred_element_type=F32)
    v_ref[...] = jnp.dot(ckv_b, wv_ref[...], preferred_element_type=F32).astype(BF16)
    gq, gk = gq_ref[...], gk_ref[...]
    for hd in range(N_HEADS):
        sl = slice(hd * LANE, (hd + 1) * LANE)
        qh = _rope_slab(q[:, sl], c, s1, s2)
        q_ref[:, sl] = _head_norm(qh, gq).astype(BF16)
        k_ref[:, sl] = _head_norm(kn[:, sl] + kr, gk).astype(BF16)


def _proj(x, tabs, tab_tiles, w, tm):
    t = x.shape[0]
    d = x.shape[1]
    assert t % tm == 0
    c_tab, s1_tab, s2_tab = tabs
    row = lambda i: (i, 0)
    tab = pl.BlockSpec((tm, LANE), lambda i: (i % tab_tiles, 0))
    in_specs = [pl.BlockSpec((tm, d), row), tab, tab, tab] + [
        _const_spec(a.shape) for a in (w["g_mix"], w["w1"], w["g_qa"], w["g_kva"], w["wq"], w["wk"],
                                       w["wv"], w["gq"], w["gk"])]
    out_shape = (
        jax.ShapeDtypeStruct((t, SLAB), BF16),
        jax.ShapeDtypeStruct((t, SLAB), BF16),
        jax.ShapeDtypeStruct((t, ATTN_WIDTH), BF16),
        jax.ShapeDtypeStruct((t, KV_LORA), F32),
        jax.ShapeDtypeStruct((t, QK_ROPE), F32),
    )
    out_specs = (
        pl.BlockSpec((tm, SLAB), row),
        pl.BlockSpec((tm, SLAB), row),
        pl.BlockSpec((tm, ATTN_WIDTH), row),
        pl.BlockSpec((tm, KV_LORA), row),
        pl.BlockSpec((tm, QK_ROPE), row),
    )
    return pl.pallas_call(
        _proj_kernel,
        grid=(t // tm,),
        in_specs=in_specs,
        out_specs=out_specs,
        out_shape=out_shape,
        compiler_params=pltpu.CompilerParams(dimension_semantics=("parallel",),
                                             vmem_limit_bytes=VMEM_LIMIT),
        name="proj",
    )(x, c_tab, s1_tab, s2_tab, w["g_mix"], w["w1"], w["g_qa"], w["g_kva"], w["wq"], w["wk"], w["wv"],
      w["gq"], w["gk"])


def _flash_kernel(q_ref, k_ref, v_ref, o_ref, *, tq, tk):
    i = pl.program_id(1)
    n_full = (i * tq) // tk
    n_diag = tq // tk
    q_chunk = (i * tq + lax.broadcasted_iota(jnp.int32, (tq, tk), 0)) // CHUNK
    k_iota = lax.broadcasted_iota(jnp.int32, (tq, tk), 1)
    low_half = lax.broadcasted_iota(jnp.int32, (tq, LANE), 1) < V_HEAD

    def head_out(hd):
        qsl = slice(hd * LANE, (hd + 1) * LANE)
        vsl = slice((hd // 2) * LANE, (hd // 2 + 1) * LANE)
        q = q_ref[:, qsl]

        def step(j, carry, masked):
            m, l, acc = carry
            start = pl.multiple_of(j * tk, tk)
            k = k_ref[pl.ds(start, tk), qsl]
            s = lax.dot_general(q, k, (((1,), (1,)), ((), ())), preferred_element_type=F32)
            if masked:
                k_chunk = (start + k_iota) // CHUNK
                s = jnp.where(k_chunk <= q_chunk, s, NEG)
            m_new = jnp.maximum(m, jnp.max(s, axis=-1, keepdims=True))
            alpha = jnp.exp(m - m_new)
            p = jnp.exp(s - m_new)
            l = alpha * l + jnp.sum(p, axis=-1, keepdims=True)
            v = v_ref[pl.ds(start, tk), vsl]
            acc = alpha * acc + jnp.dot(p.astype(BF16), v, preferred_element_type=F32)
            return m_new, l, acc

        carry = (jnp.full((tq, 1), NEG, F32), jnp.zeros((tq, 1), F32), jnp.zeros((tq, LANE), F32))
        carry = lax.fori_loop(0, n_full, functools.partial(step, masked=False), carry)
        for d in range(n_diag):
            carry = step(n_full + d, carry, True)
        _, l, acc = carry
        return acc / l

    for hp in range(N_HEADS // 2):
        o = jnp.where(low_half, head_out(2 * hp), head_out(2 * hp + 1))
        o_ref[:, hp * LANE:(hp + 1) * LANE] = o.astype(BF16)


def _flash(q, k, v, batch, seq, tq, tk):
    assert seq % tq == 0 and tq % tk == 0 and tk % CHUNK == 0
    nq = seq // tq
    k3 = k.reshape(batch, seq, SLAB)
    v3 = v.reshape(batch, seq, ATTN_WIDTH)
    return pl.pallas_call(
        functools.partial(_flash_kernel, tq=tq, tk=tk),
        grid=(batch, nq),
        in_specs=[
            pl.BlockSpec((tq, SLAB), lambda b, i: (b * nq + i, 0)),
            pl.BlockSpec((None, seq, SLAB), lambda b, i: (b, 0, 0)),
            pl.BlockSpec((None, seq, ATTN_WIDTH), lambda b, i: (b, 0, 0)),
        ],
        out_specs=pl.BlockSpec((tq, ATTN_WIDTH), lambda b, i: (b * nq + i, 0)),
        out_shape=jax.ShapeDtypeStruct((batch * seq, ATTN_WIDTH), BF16),
        compiler_params=pltpu.CompilerParams(dimension_semantics=("parallel", "parallel"),
                                             vmem_limit_bytes=VMEM_LIMIT),
        name="flash",
    )(q, k3, v3)


def _cache_attn_kernel(q_ref, cache_ref, ckr_ref, cnew_ref, krnew_ref, gk_ref, wc_ref, wuk_ref, e_ref,
                       wuv_ref, o_ref, s_ref, *, past, t_new, tk):
    n_tiles = past // tk
    ncol = N_HEADS * t_new
    gk = gk_ref[...]
    qt, qr = [], []
    for hd in range(N_HEADS):
        qg = q_ref[:, hd * LANE:(hd + 1) * LANE].astype(F32) * gk
        qt.append(jnp.dot(qg.astype(BF16), wc_ref[hd], preferred_element_type=F32))
        qr.append(qg[:, :QK_ROPE])
    qt = jnp.concatenate(qt, axis=0).astype(BF16)
    qr = jnp.concatenate(qr, axis=0).astype(BF16)

    def scores(c, kr):
        cb = c.astype(BF16)
        kn = jnp.dot(cb, wuk_ref[...], preferred_element_type=F32)
        nsq = kn * kn
        hi = nsq.astype(BF16)
        lo = (nsq - hi.astype(F32)).astype(BF16)
        e = e_ref[...]
        ssq = jnp.dot(hi, e, preferred_element_type=F32) + jnp.dot(lo, e, preferred_element_type=F32)
        ssq = ssq + jnp.sum(kr * kr, axis=-1, keepdims=True)
        inv = lax.rsqrt(ssq * (1.0 / QK_HEAD) + EPS)
        nt = (((1,), (1,)), ((), ()))
        s = lax.dot_general(cb, qt, nt, preferred_element_type=F32)
        s = s + lax.dot_general(kr.astype(BF16), qr, nt, preferred_element_type=F32)
        return s * inv

    def pass1(j, m):
        start = pl.multiple_of(j * tk, tk)
        s = scores(cache_ref[pl.ds(start, tk), :], ckr_ref[pl.ds(start, tk), :])
        s_ref[pl.ds(start, tk), :] = s
        return jnp.maximum(m, jnp.max(s, axis=0, keepdims=True))

    m = lax.fori_loop(0, n_tiles, pass1, jnp.full((1, ncol), NEG, F32))
    c_new = cnew_ref[...]
    s_new = scores(c_new, krnew_ref[...])
    m = jnp.maximum(m, jnp.max(s_new, axis=0, keepdims=True))

    tn = (((0,), (0,)), ((), ()))

    def accumulate(carry, s, c):
        l, ctx = carry
        p = jnp.exp(s - m)
        l = l + jnp.sum(p, axis=0, keepdims=True)
        ctx = ctx + lax.dot_general(p.astype(BF16), c.astype(BF16), tn, preferred_element_type=F32)
        return l, ctx

    def pass2(j, carry):
        start = pl.multiple_of(j * tk, tk)
        return accumulate(carry, s_ref[pl.ds(start, tk), :], cache_ref[pl.ds(start, tk), :])

    carry = (jnp.zeros((1, ncol), F32), jnp.zeros((ncol, KV_LORA), F32))
    carry = lax.fori_loop(0, n_tiles, pass2, carry)
    l, ctx = accumulate(carry, s_new, c_new)

    eye = lax.broadcasted_iota(jnp.int32, (ncol, ncol), 0) == lax.broadcasted_iota(jnp.int32, (ncol, ncol), 1)
    l_col = jnp.sum(jnp.where(eye, jnp.broadcast_to(l, (ncol, ncol)), 0.0), axis=1, keepdims=True)
    ctx = (ctx / l_col).astype(BF16)
    r = jnp.dot(ctx, wuv_ref[...], preferred_element_type=F32)
    col_head = lax.broadcasted_iota(jnp.int32, (t_new, ATTN_WIDTH), 1) // V_HEAD
    out = jnp.zeros((t_new, ATTN_WIDTH), F32)
    for hd in range(N_HEADS):
        out = out + jnp.where(col_head == hd, r[hd * t_new:(hd + 1) * t_new, :], 0.0)
    o_ref[...] = out.astype(BF16)


def _cache_attn(q, cache_ckv, cache_krope, layer, c_new, kr_new, w, nseq, t_new, tk):
    past = cache_ckv.shape[2]
    assert past % tk == 0
    ncol = N_HEADS * t_new
    return pl.pallas_call(
        functools.partial(_cache_attn_kernel, past=past, t_new=t_new, tk=tk),
        grid=(nseq,),
        in_specs=[
            pl.BlockSpec((t_new, SLAB), lambda b: (b, 0)),
            pl.BlockSpec((None, None, past, KV_LORA), lambda b: (layer, b, 0, 0)),
            pl.BlockSpec((None, None, past, QK_ROPE), lambda b: (layer, b, 0, 0)),
            pl.BlockSpec((t_new, KV_LORA), lambda b: (b, 0)),
            pl.BlockSpec((t_new, QK_ROPE), lambda b: (b, 0)),
        ] + [_const_spec(a.shape) for a in (w["gk1"], w["wc"], w["wuk"], w["e"], w["wv"])],
        out_specs=pl.BlockSpec((t_new, ATTN_WIDTH), lambda b: (b, 0)),
        out_shape=jax.ShapeDtypeStruct((nseq * t_new, ATTN_WIDTH), BF16),
        scratch_shapes=[pltpu.VMEM((past, ncol), F32)],
        compiler_params=pltpu.CompilerParams(dimension_semantics=("parallel",),
                                             vmem_limit_bytes=VMEM_LIMIT),
        name="cache_attn",
    )(q, cache_ckv, cache_krope, c_new, kr_new, w["gk1"], w["wc"], w["wuk"], w["e"], w["wv"])


def _merge_kernel(*refs, groups, rows, tiles_per_seq, pos0, has_hist):
    if has_hist:
        x_ref, a_ref, hist_ref = refs[:3]
        refs = refs[3:]
    else:
        x_ref, a_ref = refs[:2]
        hist_ref = None
        refs = refs[2:]
    (gmix_ref, w2_ref, wpool_ref, pscale_ref, wao_ref, wpo_ref, wo_ref, gmlp_ref, wup_ref, wdown_ref,
     y_ref, pool_ref, ext_ref) = refs
    tm = groups * rows
    tile = pl.program_id(0) % tiles_per_seq

    x = x_ref[...]
    h = _rms(x, gmix_ref[...]).astype(BF16)
    z = jnp.dot(h, w2_ref[...], preferred_element_type=F32)
    d = x.shape[1]
    p = z[:, :POOL_WIDTH]
    p3 = p.reshape(groups, rows, POOL_WIDTH)

    if has_hist:
        ext_ref[:, :POOL_PAD, :] = hist_ref[...]
    else:
        @pl.when(tile == 0)
        def _():
            ext_ref[:, :POOL_PAD, :] = jnp.zeros((groups, POOL_PAD, POOL_WIDTH), F32)

        @pl.when(tile != 0)
        def _():
            ext_ref[:, :POOL_PAD, :] = ext_ref[:, rows:rows + POOL_PAD, :]
    ext_ref[:, POOL_PAD:, :] = p3
    pool_ref[...] = p3[:, rows - POOL_PAD:, :]

    pos = pos0 + tile * rows + lax.broadcasted_iota(jnp.int32, (groups, rows, POOL_GROUP), 1)
    us = []
    for g, win in enumerate(POOL_WINDOWS):
        sl = slice(g * POOL_GROUP, (g + 1) * POOL_GROUP)
        tot = p3[:, :, sl]
        for j in range(1, win):
            tot = tot + ext_ref[:, pl.ds(POOL_PAD - j, rows), sl]
        cnt = jnp.minimum(win, pos + 1).astype(F32)
        pooled = (tot / cnt - p3[:, :, sl]).reshape(tm, POOL_GROUP)
        u = jnp.dot(pooled.astype(BF16), wpool_ref[g], preferred_element_type=F32)
        us.append((u * pscale_ref[:, sl]).astype(BF16))
    u = jnp.concatenate(us, axis=1)

    branch_a = jnp.dot(a_ref[...], wao_ref[...], preferred_element_type=F32)
    branch_b = jnp.dot(u, wpo_ref[...], preferred_element_type=F32)
    gate_a = 1.0 / (1.0 + jnp.exp(-z[:, POOL_WIDTH:POOL_WIDTH + d]))
    gate_b = 1.0 / (1.0 + jnp.exp(-z[:, POOL_WIDTH + d:]))
    mix = (gate_a * branch_a + gate_b * branch_b).astype(BF16)
    x = x + jnp.dot(mix, wo_ref[...], preferred_element_type=F32)
    hm = _rms(x, gmlp_ref[...]).astype(BF16)
    up = jnp.dot(hm, wup_ref[...], preferred_element_type=F32)
    act = jnp.square(jnp.maximum(up, 0.0)).astype(BF16)
    y_ref[...] = x + jnp.dot(act, wdown_ref[...], preferred_element_type=F32)


def _merge(x, attn, hist, w, nseq, seq, tm, pos0):
    t, d = x.shape
    if hist is None:
        assert seq % tm == 0
        groups, rows, tiles_per_seq = 1, tm, seq // tm
    else:
        assert tm % seq == 0 and seq >= POOL_PAD
        groups, rows, tiles_per_seq = tm // seq, seq, 1
    assert rows % 8 == 0 and rows >= POOL_PAD
    row = lambda i: (i, 0)
    seq_block = lambda i: (i // tiles_per_seq, 0, 0)
    in_specs = [pl.BlockSpec((tm, d), row), pl.BlockSpec((tm, ATTN_WIDTH), row)]
    args = [x, attn]
    if hist is not None:
        in_specs.append(pl.BlockSpec((groups, POOL_PAD, POOL_WIDTH), seq_block))
        args.append(hist)
    weights = (w["g_mix"], w["w2"], w["w_pool"], w["pool_scale"], w["w_attn_out"], w["w_pool_out"],
               w["w_o"], w["g_mlp"], w["w_up"], w["w_down"])
    in_specs += [_const_spec(a.shape) for a in weights]
    return pl.pallas_call(
        functools.partial(_merge_kernel, groups=groups, rows=rows, tiles_per_seq=tiles_per_seq,
                          pos0=pos0, has_hist=hist is not None),
        grid=(t // tm,),
        in_specs=in_specs,
        out_specs=(pl.BlockSpec((tm, d), row),
                   pl.BlockSpec((groups, POOL_PAD, POOL_WIDTH), seq_block)),
        out_shape=(jax.ShapeDtypeStruct((t, d), F32),
                   jax.ShapeDtypeStruct((nseq, POOL_PAD, POOL_WIDTH), F32)),
        scratch_shapes=[pltpu.VMEM((groups, POOL_PAD + rows, POOL_WIDTH), F32)],
        compiler_params=pltpu.CompilerParams(dimension_semantics=("arbitrary",),
                                             vmem_limit_bytes=VMEM_LIMIT),
        name="merge",
    )(*args, *weights)


def _rope_tables(pos):
    half = QK_ROPE // 2
    inv = jnp.power(ROPE_THETA, -jnp.arange(half, dtype=F32) / half)
    ang = pos[:, None] * inv[None, :]
    cos, sin = jnp.cos(ang), jnp.sin(ang)
    n = pos.shape[0]
    one = jnp.ones((n, LANE - QK_ROPE), F32)
    zero = jnp.zeros((n, LANE - half), F32)
    c = jnp.concatenate([cos, cos, one], axis=1)
    s1 = jnp.concatenate([jnp.zeros((n, half), F32), sin, jnp.zeros((n, LANE - QK_ROPE), F32)], axis=1)
    s2 = jnp.concatenate([-sin, zero], axis=1)
    return c, s1, s2


def _slab(rope_part, nope_part):
    pad = jnp.zeros(rope_part.shape[:-1] + (LANE - QK_HEAD,), rope_part.dtype)
    s = jnp.concatenate([rope_part, nope_part, pad], axis=-1)
    return s.reshape(s.shape[:-2] + (N_HEADS * LANE,))


def _layer_weights(l, t_new, g_mix, w_in, g_qa, g_kva, w_uq, w_ukv, g_q, g_k, w_attn_out, w_pool,
                   pool_scale, w_pool_out, w_o, g_mlp, w_up, w_down):
    off_kv, off_kr = Q_LORA, Q_LORA + KV_LORA
    off_p = off_kr + QK_ROPE
    wi = w_in[l]
    d = wi.shape[0]
    w1 = jnp.concatenate([wi[:, :off_p], jnp.zeros((d, LANE - QK_ROPE), F32)], axis=1)
    uq = w_uq[l]
    ukv = w_ukv[l]
    k_nope, v = ukv[..., :QK_NOPE], ukv[..., QK_NOPE:]
    zr = jnp.zeros(k_nope.shape[:-1] + (QK_ROPE,), F32)
    gq, gk = g_q[l], g_k[l]
    slab_gain = lambda g: jnp.concatenate([g[QK_NOPE:], g[:QK_NOPE], jnp.zeros((LANE - QK_HEAD,), F32)])[None]
    wc = jnp.zeros((N_HEADS, LANE, KV_LORA), F32).at[:, QK_ROPE:QK_HEAD, :].set(k_nope.transpose(1, 2, 0))
    ncol = N_HEADS * t_new
    e = (jnp.arange(N_HEADS * QK_NOPE)[:, None] // QK_NOPE == jnp.arange(ncol)[None, :] // t_new)
    return {
        "g_mix": g_mix[l][None],
        "w1": w1.astype(BF16),
        "w2": wi[:, off_p:].astype(BF16),
        "g_qa": g_qa[l][None],
        "g_kva": g_kva[l][None],
        "wq": _slab(uq[..., QK_NOPE:], uq[..., :QK_NOPE]).astype(BF16),
        "wk": _slab(zr, k_nope).astype(BF16),
        "wv": v.reshape(KV_LORA, ATTN_WIDTH).astype(BF16),
        "wuk": k_nope.reshape(KV_LORA, N_HEADS * QK_NOPE).astype(BF16),
        "wc": wc.astype(BF16),
        "e": e.astype(BF16),
        "gq": slab_gain(gq) * SM_SCALE,
        "gk": slab_gain(gk),
        "gk1": slab_gain(gk),
        "w_attn_out": w_attn_out[l].astype(BF16),
        "w_pool": w_pool[l].astype(BF16),
        "pool_scale": pool_scale[l][None],
        "w_pool_out": w_pool_out[l].astype(BF16),
        "w_o": w_o[l].astype(BF16),
        "g_mlp": g_mlp[l][None],
        "w_up": w_up[l].astype(BF16),
        "w_down": w_down[l].astype(BF16),
    }


def kernel(x_prompt, x_sample, cache_ckv, cache_krope, state_pool, g_mix, w_in, g_qa, g_kva, w_uq, w_ukv,
           g_q, g_k, w_attn_out, w_pool, pool_scale, w_pool_out, w_o, g_mlp, w_up, w_down):
    batch, seq, d = x_prompt.shape
    nseq, t_new, _ = x_sample.shape
    depth = g_mix.shape[0]
    past = cache_ckv.shape[2]

    tabs_p = _rope_tables(jnp.arange(seq, dtype=F32))
    proj_tm_s = min(PROJ_TM, nseq * t_new)
    tabs_s = tuple(jnp.tile(a, (proj_tm_s // t_new, 1))
                   for a in _rope_tables(past + jnp.arange(t_new, dtype=F32)))
    hist = jnp.pad(state_pool, ((0, 0), (0, 0), (POOL_PAD - POOL_HIST, 0), (0, 0)))

    xp = x_prompt.reshape(batch * seq, d)
    xs = x_sample.reshape(nseq * t_new, d)
    outs = {k: [] for k in ("ckv_p", "kr_p", "pool_p", "ckv_s", "kr_s", "pool_s")}
    for l in range(depth):
        w = _layer_weights(l, t_new, g_mix, w_in, g_qa, g_kva, w_uq, w_ukv, g_q, g_k, w_attn_out, w_pool,
                           pool_scale, w_pool_out, w_o, g_mlp, w_up, w_down)
        q, k, v, ckv, kr = _proj(xp, tabs_p, seq // PROJ_TM, w, PROJ_TM)
        attn = _flash(q, k, v, batch, seq, ATT_TQ, ATT_TK)
        xp, pool = _merge(xp, attn, None, w, batch, seq, MERGE_TM, 0)
        outs["ckv_p"].append(ckv.reshape(batch, seq, KV_LORA))
        outs["kr_p"].append(kr.reshape(batch, seq, QK_ROPE))
        outs["pool_p"].append(pool[:, POOL_PAD - POOL_HIST:])

        q, _, _, ckv, kr = _proj(xs, tabs_s, 1, w, proj_tm_s)
        attn = _cache_attn(q, cache_ckv, cache_krope, l, ckv, kr, w, nseq, t_new, CACHE_TK)
        xs, pool = _merge(xs, attn, hist[l], w, nseq, t_new, MERGE_TM, past)
        outs["ckv_s"].append(ckv.reshape(nseq, t_new, KV_LORA))
        outs["kr_s"].append(kr.reshape(nseq, t_new, QK_ROPE))
        outs["pool_s"].append(pool[:, POOL_PAD - POOL_HIST:])

    return (xp.reshape(batch, seq, d), xs.reshape(nseq, t_new, d),
            jnp.stack(outs["ckv_p"]), jnp.stack(outs["kr_p"]), jnp.stack(outs["pool_p"]),
            jnp.stack(outs["ckv_s"]), jnp.stack(outs["kr_s"]), jnp.stack(outs["pool_s"]))
```

```python
import functools

import jax
import jax.numpy as jnp
from jax import lax
from jax.experimental import pallas as pl
from jax.experimental.pallas import tpu as pltpu

F32 = jnp.float32
BF16 = jnp.bfloat16

LANE = 128
CHUNK = 64
N_HEADS = 8
QK_NOPE = 64
QK_ROPE = 32
QK_HEAD = QK_NOPE + QK_ROPE
V_HEAD = 64
Q_LORA = 384
KV_LORA = 256
ATTN_WIDTH = N_HEADS * V_HEAD
VT_ROWS = V_HEAD + 16
POOL_WINDOWS = (2, 4, 8, 16)
POOL_GROUP = 128
POOL_WIDTH = len(POOL_WINDOWS) * POOL_GROUP
POOL_HIST = max(POOL_WINDOWS) - 1
POOL_PAD = POOL_HIST + 1
ROPE_THETA = 10000.0
EPS = 1e-6
SM_SCALE = QK_HEAD ** -0.5
LOG2E = 1.4426950408889634
SLAB = N_HEADS * LANE
NEG = -1e30

PROJ_TM = 512
ATT_T = 256
MERGE_TM = 256
CACHE_TK = 512
VMEM_LIMIT = 56 * 1024 * 1024


def _const_spec(shape):
    nd = len(shape)
    return pl.BlockSpec(shape, lambda *_: (0,) * nd, pipeline_mode=pl.Buffered(1))


def _rms(x, g):
    ms = jnp.mean(x * x, axis=-1, keepdims=True)
    return x * lax.rsqrt(ms + EPS) * g


def _rope_slab(x, c, s1, s2):
    return x * c + pltpu.roll(x, 16, 1) * s1 + pltpu.roll(x, LANE - 16, 1) * s2


def _head_norm(x, g):
    ss = jnp.sum(x * x, axis=-1, keepdims=True)
    return x * lax.rsqrt(ss * (1.0 / QK_HEAD) + EPS) * g


def _proj_kernel(x_ref, c_ref, s1_ref, s2_ref, gmix_ref, w1_ref, gqa_ref, gkva_ref, wq_ref, wk_ref,
                 wv_ref, gq_ref, gk_ref, *out_refs, transposed):
    if transposed:
        q_ref, k_ref, v_ref, ckv_ref, kr_ref = out_refs
    else:
        q_ref, ckv_ref, kr_ref = out_refs
    x = x_ref[...]
    tm = x.shape[0]
    h = _rms(x, gmix_ref[...]).astype(BF16)
    z = jnp.dot(h, w1_ref[...], preferred_element_type=F32)
    cq = _rms(z[:, :Q_LORA], gqa_ref[...]).astype(BF16)
    ckv = _rms(z[:, Q_LORA:Q_LORA + KV_LORA], gkva_ref[...])
    ckv_ref[...] = ckv
    c, s1, s2 = c_ref[...], s1_ref[...], s2_ref[...]
    kr = _rope_slab(z[:, Q_LORA + KV_LORA:], c, s1, s2)
    kr_ref[...] = kr[:, :QK_ROPE]
    q = jnp.dot(cq, wq_ref[...], preferred_element_type=F32)
    gq = gq_ref[...]
    if not transposed:
        for hd in range(N_HEADS):
            sl = slice(hd * LANE, (hd + 1) * LANE)
            q_ref[:, sl] = _head_norm(_rope_slab(q[:, sl], c, s1, s2), gq).astype(BF16)
        return
    ckv_b = ckv.astype(BF16)
    kn = jnp.dot(ckv_b, wk_ref[...], preferred_element_type=F32)
    v = jnp.dot(ckv_b, wv_ref[...], preferred_element_type=F32)
    gk = gk_ref[...]
    ones = jnp.ones((VT_ROWS - V_HEAD, ATT_T), BF16)
    for t in range(tm // ATT_T):
        vt = v[t * ATT_T:(t + 1) * ATT_T, :].T.astype(BF16)
        for hd in range(N_HEADS):
            v_ref[t, hd * VT_ROWS:hd * VT_ROWS + V_HEAD, :] = vt[hd * V_HEAD:(hd + 1) * V_HEAD, :]
            v_ref[t, hd * VT_ROWS + V_HEAD:(hd + 1) * VT_ROWS, :] = ones
    for hd in range(N_HEADS):
        sl = slice(hd * LANE, (hd + 1) * LANE)
        qh = _head_norm(_rope_slab(q[:, sl], c, s1, s2), gq)
        for t in range(tm // ATT_T):
            q_ref[t, sl, :] = qh[t * ATT_T:(t + 1) * ATT_T, :].T.astype(BF16)
        k_ref[:, sl] = _head_norm(kn[:, sl] + kr, gk).astype(BF16)


def _proj(x, tabs, tab_tiles, w, tm, transposed):
    t = x.shape[0]
    d = x.shape[1]
    assert t % tm == 0 and tm % ATT_T == 0
    c_tab, s1_tab, s2_tab = tabs
    row = lambda i: (i, 0)
    lead = lambda i: (i, 0, 0)
    tab = pl.BlockSpec((tm, LANE), lambda i: (i % tab_tiles, 0))
    in_specs = [pl.BlockSpec((tm, d), row), tab, tab, tab] + [
        _const_spec(a.shape) for a in (w["g_mix"], w["w1"], w["g_qa"], w["g_kva"], w["wq"], w["wk"],
                                       w["wv"], w["gq"], w["gk"])]
    latent_shapes = (jax.ShapeDtypeStruct((t, KV_LORA), F32), jax.ShapeDtypeStruct((t, QK_ROPE), F32))
    latent_specs = (pl.BlockSpec((tm, KV_LORA), row), pl.BlockSpec((tm, QK_ROPE), row))
    if transposed:
        nt = tm // ATT_T
        out_shape = (
            jax.ShapeDtypeStruct((t // ATT_T, SLAB, ATT_T), BF16),
            jax.ShapeDtypeStruct((t, SLAB), BF16),
            jax.ShapeDtypeStruct((t // ATT_T, N_HEADS * VT_ROWS, ATT_T), BF16),
        ) + latent_shapes
        out_specs = (
            pl.BlockSpec((nt, SLAB, ATT_T), lead),
            pl.BlockSpec((tm, SLAB), row),
            pl.BlockSpec((nt, N_HEADS * VT_ROWS, ATT_T), lead),
        ) + latent_specs
    else:
        out_shape = (jax.ShapeDtypeStruct((t, SLAB), BF16),) + latent_shapes
        out_specs = (pl.BlockSpec((tm, SLAB), row),) + latent_specs
    return pl.pallas_call(
        functools.partial(_proj_kernel, transposed=transposed),
        grid=(t // tm,),
        in_specs=in_specs,
        out_specs=out_specs,
        out_shape=out_shape,
        compiler_params=pltpu.CompilerParams(dimension_semantics=("parallel",),
                                             vmem_limit_bytes=VMEM_LIMIT),
        name="proj",
    )(x, c_tab, s1_tab, s2_tab, w["g_mix"], w["w1"], w["g_qa"], w["g_kva"], w["wq"], w["wk"], w["wv"],
      w["gq"], w["gk"])


def _flash_kernel(qt_ref, k_ref, vt_ref, o_ref, m_ref, acc_ref, s_ref, mt_ref):
    t = ATT_T
    i = pl.program_id(1)
    m_ref[...] = jnp.full(m_ref.shape, NEG, F32)
    acc_ref[...] = jnp.zeros(acc_ref.shape, F32)

    def score_phase(slot, tile, masked):
        start = pl.multiple_of(tile * t, t)
        for hd in range(N_HEADS):
            qsl = slice(hd * LANE, (hd + 1) * LANE)
            s = jnp.dot(k_ref[pl.ds(start, t), qsl], qt_ref[qsl, :], preferred_element_type=F32)
            if masked:
                k_chunk = lax.broadcasted_iota(jnp.int32, (t, t), 0) // CHUNK
                q_chunk = lax.broadcasted_iota(jnp.int32, (t, t), 1) // CHUNK
                s = jnp.where(k_chunk <= q_chunk, s, NEG)
            s_ref[slot, hd] = s
            mt_ref[slot, hd:hd + 1, :] = jnp.max(s, axis=0, keepdims=True)

    def value_phase(slot, tile):
        for hd in range(N_HEADS):
            vsl = slice(hd * VT_ROWS, (hd + 1) * VT_ROWS)
            m_old = m_ref[hd:hd + 1, :]
            m_new = jnp.maximum(m_old, mt_ref[slot, hd:hd + 1, :])
            alpha = jnp.exp2(m_old - m_new)
            p = jnp.exp2(s_ref[slot, hd] - m_new).astype(BF16)
            pv = jnp.dot(vt_ref[tile, vsl, :], p, preferred_element_type=F32)
            acc_ref[vsl, :] = alpha * acc_ref[vsl, :] + pv
            m_ref[hd:hd + 1, :] = m_new

    score_phase(0, i, True)

    def body(j, carry):
        cur = j % 2
        score_phase(1 - cur, j, False)
        value_phase(cur, jnp.where(j == 0, i, j - 1))
        return carry

    lax.fori_loop(0, i, body, 0)
    value_phase(i % 2, jnp.where(i == 0, i, i - 1))
    outs = []
    for hd in range(N_HEADS):
        base = hd * VT_ROWS
        outs.append(acc_ref[base:base + V_HEAD, :] / acc_ref[base + V_HEAD:base + V_HEAD + 1, :])
    o_ref[...] = jnp.concatenate(outs, axis=0).T.astype(BF16)


def _flash(qt, k, vt, batch, seq):
    assert seq % ATT_T == 0 and ATT_T % CHUNK == 0
    nq = seq // ATT_T
    k3 = k.reshape(batch, seq, SLAB)
    return pl.pallas_call(
        _flash_kernel,
        grid=(batch, nq),
        in_specs=[
            pl.BlockSpec((None, SLAB, ATT_T), lambda b, i: (b * nq + i, 0, 0)),
            pl.BlockSpec((None, seq, SLAB), lambda b, i: (b, 0, 0)),
            pl.BlockSpec((nq, N_HEADS * VT_ROWS, ATT_T), lambda b, i: (b, 0, 0)),
        ],
        out_specs=pl.BlockSpec((ATT_T, ATTN_WIDTH), lambda b, i: (b * nq + i, 0)),
        out_shape=jax.ShapeDtypeStruct((batch * seq, ATTN_WIDTH), BF16),
        scratch_shapes=[pltpu.VMEM((N_HEADS, ATT_T), F32), pltpu.VMEM((N_HEADS * VT_ROWS, ATT_T), F32),
                        pltpu.VMEM((2, N_HEADS, ATT_T, ATT_T), F32), pltpu.VMEM((2, N_HEADS, ATT_T), F32)],
        compiler_params=pltpu.CompilerParams(dimension_semantics=("parallel", "parallel"),
                                             vmem_limit_bytes=VMEM_LIMIT),
        name="flash",
    )(qt, k3, vt)


def _cache_attn_kernel(q_ref, cache_ref, ckr_ref, cnew_ref, krnew_ref, gk_ref, wc_ref, wuk_ref, e_ref,
                       wuv_ref, o_ref, s_ref, *, past, t_new, tk):
    n_tiles = past // tk
    ncol = N_HEADS * t_new
    gk = gk_ref[...]
    qt, qr = [], []
    for hd in range(N_HEADS):
        qg = q_ref[:, hd * LANE:(hd + 1) * LANE].astype(F32) * gk
        qt.append(jnp.dot(qg.astype(BF16), wc_ref[hd], preferred_element_type=F32))
        qr.append(qg[:, :QK_ROPE])
    qt = jnp.concatenate(qt, axis=0).astype(BF16)
    qr = jnp.concatenate(qr, axis=0).astype(BF16)

    def scores(c, kr):
        cb = c.astype(BF16)
        kn = jnp.dot(cb, wuk_ref[...], preferred_element_type=F32)
        nsq = kn * kn
        hi = nsq.astype(BF16)
        lo = (nsq - hi.astype(F32)).astype(BF16)
        e = e_ref[...]
        ssq = jnp.dot(hi, e, preferred_element_type=F32) + jnp.dot(lo, e, preferred_element_type=F32)
        ssq = ssq + jnp.sum(kr * kr, axis=-1, keepdims=True)
        inv = lax.rsqrt(ssq * (1.0 / QK_HEAD) + EPS)
        nt = (((1,), (1,)), ((), ()))
        s = lax.dot_general(cb, qt, nt, preferred_element_type=F32)
        s = s + lax.dot_general(kr.astype(BF16), qr, nt, preferred_element_type=F32)
        return s * inv

    def pass1(j, m):
        start = pl.multiple_of(j * tk, tk)
        s = scores(cache_ref[pl.ds(start, tk), :], ckr_ref[pl.ds(start, tk), :])
        s_ref[pl.ds(start, tk), :] = s
        return jnp.maximum(m, jnp.max(s, axis=0, keepdims=True))

    m = lax.fori_loop(0, n_tiles, pass1, jnp.full((1, ncol), NEG, F32))
    c_new = cnew_ref[...]
    s_new = scores(c_new, krnew_ref[...])
    m = jnp.maximum(m, jnp.max(s_new, axis=0, keepdims=True))

    tn = (((0,), (0,)), ((), ()))

    def accumulate(carry, s, c):
        l, ctx = carry
        p = jnp.exp2(s - m)
        l = l + jnp.sum(p, axis=0, keepdims=True)
        ctx = ctx + lax.dot_general(p.astype(BF16), c.astype(BF16), tn, preferred_element_type=F32)
        return l, ctx

    def pass2(j, carry):
        start = pl.multiple_of(j * tk, tk)
        return accumulate(carry, s_ref[pl.ds(start, tk), :], cache_ref[pl.ds(start, tk), :])

    carry = (jnp.zeros((1, ncol), F32), jnp.zeros((ncol, KV_LORA), F32))
    carry = lax.fori_loop(0, n_tiles, pass2, carry)
    l, ctx = accumulate(carry, s_new, c_new)

    eye = lax.broadcasted_iota(jnp.int32, (ncol, ncol), 0) == lax.broadcasted_iota(jnp.int32, (ncol, ncol), 1)
    l_col = jnp.sum(jnp.where(eye, jnp.broadcast_to(l, (ncol, ncol)), 0.0), axis=1, keepdims=True)
    ctx = (ctx / l_col).astype(BF16)
    r = jnp.dot(ctx, wuv_ref[...], preferred_element_type=F32)
    col_head = lax.broadcasted_iota(jnp.int32, (t_new, ATTN_WIDTH), 1) // V_HEAD
    out = jnp.zeros((t_new, ATTN_WIDTH), F32)
    for hd in range(N_HEADS):
        out = out + jnp.where(col_head == hd, r[hd * t_new:(hd + 1) * t_new, :], 0.0)
    o_ref[...] = out.astype(BF16)


def _cache_attn(q, cache_ckv, cache_krope, layer, c_new, kr_new, w, nseq, t_new, tk):
    past = cache_ckv.shape[2]
    assert past % tk == 0
    ncol = N_HEADS * t_new
    return pl.pallas_call(
        functools.partial(_cache_attn_kernel, past=past, t_new=t_new, tk=tk),
        grid=(nseq,),
        in_specs=[
            pl.BlockSpec((t_new, SLAB), lambda b: (b, 0)),
            pl.BlockSpec((None, None, past, KV_LORA), lambda b: (layer, b, 0, 0)),
            pl.BlockSpec((None, None, past, QK_ROPE), lambda b: (layer, b, 0, 0)),
            pl.BlockSpec((t_new, KV_LORA), lambda b: (b, 0)),
            pl.BlockSpec((t_new, QK_ROPE), lambda b: (b, 0)),
        ] + [_const_spec(a.shape) for a in (w["gk1"], w["wc"], w["wuk"], w["e"], w["wv"])],
        out_specs=pl.BlockSpec((t_new, ATTN_WIDTH), lambda b: (b, 0)),
        out_shape=jax.ShapeDtypeStruct((nseq * t_new, ATTN_WIDTH), BF16),
        scratch_shapes=[pltpu.VMEM((past, ncol), F32)],
        compiler_params=pltpu.CompilerParams(dimension_semantics=("parallel",),
                                             vmem_limit_bytes=VMEM_LIMIT),
        name="cache_attn",
    )(q, cache_ckv, cache_krope, c_new, kr_new, w["gk1"], w["wc"], w["wuk"], w["e"], w["wv"])


def _merge_kernel(*refs, groups, rows, tiles_per_seq, pos0, has_hist):
    if has_hist:
        x_ref, a_ref, hist_ref = refs[:3]
        refs = refs[3:]
    else:
        x_ref, a_ref = refs[:2]
        hist_ref = None
        refs = refs[2:]
    (gmix_ref, w2_ref, wpool_ref, pscale_ref, wao_ref, wpo_ref, wo_ref, gmlp_ref, wup_ref, wdown_ref,
     y_ref, pool_ref, ext_ref) = refs
    tm = groups * rows
    tile = pl.program_id(0) % tiles_per_seq

    x = x_ref[...]
    h = _rms(x, gmix_ref[...]).astype(BF16)
    z = jnp.dot(h, w2_ref[...], preferred_element_type=F32)
    d = x.shape[1]
    p = z[:, :POOL_WIDTH]
    p3 = p.reshape(groups, rows, POOL_WIDTH)

    if has_hist:
        ext_ref[:, :POOL_PAD, :] = hist_ref[...]
    else:
        @pl.when(tile == 0)
        def _():
            ext_ref[:, :POOL_PAD, :] = jnp.zeros((groups, POOL_PAD, POOL_WIDTH), F32)

        @pl.when(tile != 0)
        def _():
            ext_ref[:, :POOL_PAD, :] = ext_ref[:, rows:rows + POOL_PAD, :]
    ext_ref[:, POOL_PAD:, :] = p3
    pool_ref[...] = p3[:, rows - POOL_PAD:, :]

    pos = pos0 + tile * rows + lax.broadcasted_iota(jnp.int32, (groups, rows, POOL_GROUP), 1)
    us = []
    for g, win in enumerate(POOL_WINDOWS):
        sl = slice(g * POOL_GROUP, (g + 1) * POOL_GROUP)
        tot = p3[:, :, sl]
        for j in range(1, win):
            tot = tot + ext_ref[:, pl.ds(POOL_PAD - j, rows), sl]
        cnt = jnp.minimum(win, pos + 1).astype(F32)
        pooled = (tot / cnt - p3[:, :, sl]).reshape(tm, POOL_GROUP)
        u = jnp.dot(pooled.astype(BF16), wpool_ref[g], preferred_element_type=F32)
        us.append((u * pscale_ref[:, sl]).astype(BF16))
    u = jnp.concatenate(us, axis=1)

    branch_a = jnp.dot(a_ref[...], wao_ref[...], preferred_element_type=F32)
    branch_b = jnp.dot(u, wpo_ref[...], preferred_element_type=F32)
    gate_a = 1.0 / (1.0 + jnp.exp(-z[:, POOL_WIDTH:POOL_WIDTH + d]))
    gate_b = 1.0 / (1.0 + jnp.exp(-z[:, POOL_WIDTH + d:]))
    mix = (gate_a * branch_a + gate_b * branch_b).astype(BF16)
    x = x + jnp.dot(mix, wo_ref[...], preferred_element_type=F32)
    hm = _rms(x, gmlp_ref[...]).astype(BF16)
    up = jnp.dot(hm, wup_ref[...], preferred_element_type=F32)
    act = jnp.square(jnp.maximum(up, 0.0)).astype(BF16)
    y_ref[...] = x + jnp.dot(act, wdown_ref[...], preferred_element_type=F32)


def _merge(x, attn, hist, w, nseq, seq, tm, pos0):
    t, d = x.shape
    if hist is None:
        assert seq % tm == 0
        groups, rows, tiles_per_seq = 1, tm, seq // tm
    else:
        assert tm % seq == 0 and seq >= POOL_PAD
        groups, rows, tiles_per_seq = tm // seq, seq, 1
    assert rows % 8 == 0 and rows >= POOL_PAD
    row = lambda i: (i, 0)
    seq_block = lambda i: (i // tiles_per_seq, 0, 0)
    in_specs = [pl.BlockSpec((tm, d), row), pl.BlockSpec((tm, ATTN_WIDTH), row)]
    args = [x, attn]
    if hist is not None:
        in_specs.append(pl.BlockSpec((groups, POOL_PAD, POOL_WIDTH), seq_block))
        args.append(hist)
    weights = (w["g_mix"], w["w2"], w["w_pool"], w["pool_scale"], w["w_attn_out"], w["w_pool_out"],
               w["w_o"], w["g_mlp"], w["w_up"], w["w_down"])
    in_specs += [_const_spec(a.shape) for a in weights]
    return pl.pallas_call(
        functools.partial(_merge_kernel, groups=groups, rows=rows, tiles_per_seq=tiles_per_seq,
                          pos0=pos0, has_hist=hist is not None),
        grid=(t // tm,),
        in_specs=in_specs,
        out_specs=(pl.BlockSpec((tm, d), row),
                   pl.BlockSpec((groups, POOL_PAD, POOL_WIDTH), seq_block)),
        out_shape=(jax.ShapeDtypeStruct((t, d), F32),
                   jax.ShapeDtypeStruct((nseq, POOL_PAD, POOL_WIDTH), F32)),
        scratch_shapes=[pltpu.VMEM((groups, POOL_PAD + rows, POOL_WIDTH), F32)],
        compiler_params=pltpu.CompilerParams(dimension_semantics=("arbitrary",),
                                             vmem_limit_bytes=VMEM_LIMIT),
        name="merge",
    )(*args, *weights)


def _rope_tables(pos):
    half = QK_ROPE // 2
    inv = jnp.power(ROPE_THETA, -jnp.arange(half, dtype=F32) / half)
    ang = pos[:, None] * inv[None, :]
    cos, sin = jnp.cos(ang), jnp.sin(ang)
    n = pos.shape[0]
    one = jnp.ones((n, LANE - QK_ROPE), F32)
    zero = jnp.zeros((n, LANE - half), F32)
    c = jnp.concatenate([cos, cos, one], axis=1)
    s1 = jnp.concatenate([jnp.zeros((n, half), F32), sin, jnp.zeros((n, LANE - QK_ROPE), F32)], axis=1)
    s2 = jnp.concatenate([-sin, zero], axis=1)
    return c, s1, s2


def _slab(rope_part, nope_part):
    pad = jnp.zeros(rope_part.shape[:-1] + (LANE - QK_HEAD,), rope_part.dtype)
    s = jnp.concatenate([rope_part, nope_part, pad], axis=-1)
    return s.reshape(s.shape[:-2] + (N_HEADS * LANE,))


def _layer_weights(l, t_new, g_mix, w_in, g_qa, g_kva, w_uq, w_ukv, g_q, g_k, w_attn_out, w_pool,
                   pool_scale, w_pool_out, w_o, g_mlp, w_up, w_down):
    off_kv, off_kr = Q_LORA, Q_LORA + KV_LORA
    off_p = off_kr + QK_ROPE
    wi = w_in[l]
    d = wi.shape[0]
    w1 = jnp.concatenate([wi[:, :off_p], jnp.zeros((d, LANE - QK_ROPE), F32)], axis=1)
    uq = w_uq[l]
    ukv = w_ukv[l]
    k_nope, v = ukv[..., :QK_NOPE], ukv[..., QK_NOPE:]
    zr = jnp.zeros(k_nope.shape[:-1] + (QK_ROPE,), F32)
    gq, gk = g_q[l], g_k[l]
    slab_gain = lambda g: jnp.concatenate([g[QK_NOPE:], g[:QK_NOPE], jnp.zeros((LANE - QK_HEAD,), F32)])[None]
    wc = jnp.zeros((N_HEADS, LANE, KV_LORA), F32).at[:, QK_ROPE:QK_HEAD, :].set(k_nope.transpose(1, 2, 0))
    ncol = N_HEADS * t_new
    e = (jnp.arange(N_HEADS * QK_NOPE)[:, None] // QK_NOPE == jnp.arange(ncol)[None, :] // t_new)
    return {
        "g_mix": g_mix[l][None],
        "w1": w1.astype(BF16),
        "w2": wi[:, off_p:].astype(BF16),
        "g_qa": g_qa[l][None],
        "g_kva": g_kva[l][None],
        "wq": _slab(uq[..., QK_NOPE:], uq[..., :QK_NOPE]).astype(BF16),
        "wk": _slab(zr, k_nope).astype(BF16),
        "wv": v.reshape(KV_LORA, ATTN_WIDTH).astype(BF16),
        "wuk": k_nope.reshape(KV_LORA, N_HEADS * QK_NOPE).astype(BF16),
        "wc": wc.astype(BF16),
        "e": e.astype(BF16),
        "gq": slab_gain(gq) * (SM_SCALE * LOG2E),
        "gk": slab_gain(gk),
        "gk1": slab_gain(gk),
        "w_attn_out": w_attn_out[l].astype(BF16),
        "w_pool": w_pool[l].astype(BF16),
        "pool_scale": pool_scale[l][None],
        "w_pool_out": w_pool_out[l].astype(BF16),
        "w_o": w_o[l].astype(BF16),
        "g_mlp": g_mlp[l][None],
        "w_up": w_up[l].astype(BF16),
        "w_down": w_down[l].astype(BF16),
    }


def kernel(x_prompt, x_sample, cache_ckv, cache_krope, state_pool, g_mix, w_in, g_qa, g_kva, w_uq, w_ukv,
           g_q, g_k, w_attn_out, w_pool, pool_scale, w_pool_out, w_o, g_mlp, w_up, w_down):
    batch, seq, d = x_prompt.shape
    nseq, t_new, _ = x_sample.shape
    depth = g_mix.shape[0]
    past = cache_ckv.shape[2]

    tabs_p = _rope_tables(jnp.arange(seq, dtype=F32))
    proj_tm_s = min(PROJ_TM, nseq * t_new)
    tabs_s = tuple(jnp.tile(a, (proj_tm_s // t_new, 1))
                   for a in _rope_tables(past + jnp.arange(t_new, dtype=F32)))
    hist = jnp.pad(state_pool, ((0, 0), (0, 0), (POOL_PAD - POOL_HIST, 0), (0, 0)))

    xp = x_prompt.reshape(batch * seq, d)
    xs = x_sample.reshape(nseq * t_new, d)
    outs = {k: [] for k in ("ckv_p", "kr_p", "pool_p", "ckv_s", "kr_s", "pool_s")}
    for l in range(depth):
        w = _layer_weights(l, t_new, g_mix, w_in, g_qa, g_kva, w_uq, w_ukv, g_q, g_k, w_attn_out, w_pool,
                           pool_scale, w_pool_out, w_o, g_mlp, w_up, w_down)
        qt, k, vt, ckv, kr = _proj(xp, tabs_p, seq // PROJ_TM, w, PROJ_TM, True)
        attn = _flash(qt, k, vt, batch, seq)
        xp, pool = _merge(xp, attn, None, w, batch, seq, MERGE_TM, 0)
        outs["ckv_p"].append(ckv.reshape(batch, seq, KV_LORA))
        outs["kr_p"].append(kr.reshape(batch, seq, QK_ROPE))
        outs["pool_p"].append(pool[:, POOL_PAD - POOL_HIST:])

        q, ckv, kr = _proj(xs, tabs_s, 1, w, proj_tm_s, False)
        attn = _cache_attn(q, cache_ckv, cache_krope, l, ckv, kr, w, nseq, t_new, CACHE_TK)
        xs, pool = _merge(xs, attn, hist[l], w, nseq, t_new, MERGE_TM, past)
        outs["ckv_s"].append(ckv.reshape(nseq, t_new, KV_LORA))
        outs["kr_s"].append(kr.reshape(nseq, t_new, QK_ROPE))
        outs["pool_s"].append(pool[:, POOL_PAD - POOL_HIST:])

    return (xp.reshape(batch, seq, d), xs.reshape(nseq, t_new, d),
            jnp.stack(outs["ckv_p"]), jnp.stack(outs["kr_p"]), jnp.stack(outs["pool_p"]),
            jnp.stack(outs["ckv_s"]), jnp.stack(outs["kr_s"]), jnp.stack(outs["pool_s"]))
```

```python
import functools

import jax
import jax.numpy as jnp
from jax import lax
from jax.experimental import pallas as pl
from jax.experimental.pallas import tpu as pltpu

F32 = jnp.float32
BF16 = jnp.bfloat16

LANE = 128
CHUNK = 64
N_HEADS = 8
QK_NOPE = 64
QK_ROPE = 32
QK_HEAD = QK_NOPE + QK_ROPE
V_HEAD = 64
Q_LORA = 384
KV_LORA = 256
ATTN_WIDTH = N_HEADS * V_HEAD
VT_ROWS = V_HEAD + 16
POOL_WINDOWS = (2, 4, 8, 16)
POOL_GROUP = 128
POOL_WIDTH = len(POOL_WINDOWS) * POOL_GROUP
POOL_HIST = max(POOL_WINDOWS) - 1
POOL_PAD = POOL_HIST + 1
ROPE_THETA = 10000.0
EPS = 1e-6
SM_SCALE = QK_HEAD ** -0.5
LOG2E = 1.4426950408889634
SLAB = N_HEADS * LANE
NEG = -1e30

PROJ_TM = 512
ATT_T = 256
MERGE_TM = 256
CACHE_TK = 512
VMEM_LIMIT = 56 * 1024 * 1024


def _const_spec(shape):
    nd = len(shape)
    return pl.BlockSpec(shape, lambda *_: (0,) * nd, pipeline_mode=pl.Buffered(1))


def _rms(x, g):
    ms = jnp.mean(x * x, axis=-1, keepdims=True)
    return x * lax.rsqrt(ms + EPS) * g


def _rope_slab(x, c, s1, s2):
    return x * c + pltpu.roll(x, 16, 1) * s1 + pltpu.roll(x, LANE - 16, 1) * s2


def _head_norm(x, g):
    ss = jnp.sum(x * x, axis=-1, keepdims=True)
    return x * lax.rsqrt(ss * (1.0 / QK_HEAD) + EPS) * g


def _proj_kernel(x_ref, c_ref, s1_ref, s2_ref, gmix_ref, w1_ref, gqa_ref, gkva_ref, wq_ref, wk_ref,
                 wv_ref, gq_ref, gk_ref, *out_refs, transposed):
    if transposed:
        q_ref, k_ref, v_ref, ckv_ref, kr_ref = out_refs
    else:
        q_ref, ckv_ref, kr_ref = out_refs
    x = x_ref[...]
    tm = x.shape[0]
    h = _rms(x, gmix_ref[...]).astype(BF16)
    z = jnp.dot(h, w1_ref[...], preferred_element_type=F32)
    cq = _rms(z[:, :Q_LORA], gqa_ref[...]).astype(BF16)
    ckv = _rms(z[:, Q_LORA:Q_LORA + KV_LORA], gkva_ref[...])
    ckv_ref[...] = ckv
    c, s1, s2 = c_ref[...], s1_ref[...], s2_ref[...]
    kr = _rope_slab(z[:, Q_LORA + KV_LORA:], c, s1, s2)
    kr_ref[...] = kr[:, :QK_ROPE]
    q = jnp.dot(cq, wq_ref[...], preferred_element_type=F32)
    gq = gq_ref[...]
    if not transposed:
        for hd in range(N_HEADS):
            sl = slice(hd * LANE, (hd + 1) * LANE)
            q_ref[:, sl] = _head_norm(_rope_slab(q[:, sl], c, s1, s2), gq).astype(BF16)
        return
    ckv_b = ckv.astype(BF16)
    kn = jnp.dot(ckv_b, wk_ref[...], preferred_element_type=F32)
    v = jnp.dot(ckv_b, wv_ref[...], preferred_element_type=F32)
    gk = gk_ref[...]
    ones = jnp.ones((VT_ROWS - V_HEAD, ATT_T), BF16)
    for t in range(tm // ATT_T):
        vt = v[t * ATT_T:(t + 1) * ATT_T, :].T.astype(BF16)
        for hd in range(N_HEADS):
            v_ref[t, hd * VT_ROWS:hd * VT_ROWS + V_HEAD, :] = vt[hd * V_HEAD:(hd + 1) * V_HEAD, :]
            v_ref[t, hd * VT_ROWS + V_HEAD:(hd + 1) * VT_ROWS, :] = ones
    for hd in range(N_HEADS):
        sl = slice(hd * LANE, (hd + 1) * LANE)
        qh = _head_norm(_rope_slab(q[:, sl], c, s1, s2), gq)
        for t in range(tm // ATT_T):
            q_ref[t, sl, :] = qh[t * ATT_T:(t + 1) * ATT_T, :].T.astype(BF16)
        k_ref[:, sl] = _head_norm(kn[:, sl] + kr, gk).astype(BF16)


def _proj(x, tabs, tab_tiles, w, tm, transposed):
    t = x.shape[0]
    d = x.shape[1]
    assert t % tm == 0 and tm % ATT_T == 0
    c_tab, s1_tab, s2_tab = tabs
    row = lambda i: (i, 0)
    lead = lambda i: (i, 0, 0)
    tab = pl.BlockSpec((tm, LANE), lambda i: (i % tab_tiles, 0))
    in_specs = [pl.BlockSpec((tm, d), row), tab, tab, tab] + [
        _const_spec(a.shape) for a in (w["g_mix"], w["w1"], w["g_qa"], w["g_kva"], w["wq"], w["wk"],
                                       w["wv"], w["gq"], w["gk"])]
    latent_shapes = (jax.ShapeDtypeStruct((t, KV_LORA), F32), jax.ShapeDtypeStruct((t, QK_ROPE), F32))
    latent_specs = (pl.BlockSpec((tm, KV_LORA), row), pl.BlockSpec((tm, QK_ROPE), row))
    if transposed:
        nt = tm // ATT_T
        out_shape = (
            jax.ShapeDtypeStruct((t // ATT_T, SLAB, ATT_T), BF16),
            jax.ShapeDtypeStruct((t, SLAB), BF16),
            jax.ShapeDtypeStruct((t // ATT_T, N_HEADS * VT_ROWS, ATT_T), BF16),
        ) + latent_shapes
        out_specs = (
            pl.BlockSpec((nt, SLAB, ATT_T), lead),
            pl.BlockSpec((tm, SLAB), row),
            pl.BlockSpec((nt, N_HEADS * VT_ROWS, ATT_T), lead),
        ) + latent_specs
    else:
        out_shape = (jax.ShapeDtypeStruct((t, SLAB), BF16),) + latent_shapes
        out_specs = (pl.BlockSpec((tm, SLAB), row),) + latent_specs
    return pl.pallas_call(
        functools.partial(_proj_kernel, transposed=transposed),
        grid=(t // tm,),
        in_specs=in_specs,
        out_specs=out_specs,
        out_shape=out_shape,
        compiler_params=pltpu.CompilerParams(dimension_semantics=("parallel",),
                                             vmem_limit_bytes=VMEM_LIMIT),
        name="proj",
    )(x, c_tab, s1_tab, s2_tab, w["g_mix"], w["w1"], w["g_qa"], w["g_kva"], w["wq"], w["wk"], w["wv"],
      w["gq"], w["gk"])


def _flash_kernel(qt_ref, k_ref, vt_ref, o_ref, m_ref, acc_ref, s_ref, mt_ref):
    t = ATT_T
    i = pl.program_id(1)
    m_ref[...] = jnp.full(m_ref.shape, NEG, F32)
    acc_ref[...] = jnp.zeros(acc_ref.shape, F32)

    def score_phase(slot, tile, masked):
        start = pl.multiple_of(tile * t, t)
        for hd in range(N_HEADS):
            qsl = slice(hd * LANE, (hd + 1) * LANE)
            s = jnp.dot(k_ref[pl.ds(start, t), qsl], qt_ref[qsl, :], preferred_element_type=F32)
            if masked:
                k_chunk = lax.broadcasted_iota(jnp.int32, (t, t), 0) // CHUNK
                q_chunk = lax.broadcasted_iota(jnp.int32, (t, t), 1) // CHUNK
                s = jnp.where(k_chunk <= q_chunk, s, NEG)
            s_ref[slot, hd] = s
            mt_ref[slot, hd:hd + 1, :] = jnp.max(s, axis=0, keepdims=True)

    def value_phase(slot, tile):
        for hd in range(N_HEADS):
            vsl = slice(hd * VT_ROWS, (hd + 1) * VT_ROWS)
            m_old = m_ref[hd:hd + 1, :]
            m_new = jnp.maximum(m_old, mt_ref[slot, hd:hd + 1, :])
            alpha = jnp.exp2(m_old - m_new)
            p = jnp.exp2(s_ref[slot, hd] - m_new).astype(BF16)
            pv = jnp.dot(vt_ref[tile, vsl, :], p, preferred_element_type=F32)
            acc_ref[vsl, :] = alpha * acc_ref[vsl, :] + pv
            m_ref[hd:hd + 1, :] = m_new

    score_phase(0, i, True)
    n_pairs = i // 2

    def pair(p, carry):
        score_phase(1, 2 * p, False)
        value_phase(0, jnp.where(p == 0, i, 2 * p - 1))
        score_phase(0, 2 * p + 1, False)
        value_phase(1, 2 * p)
        return carry

    lax.fori_loop(0, n_pairs, pair, 0)
    pending = jnp.where(n_pairs == 0, i, 2 * n_pairs - 1)

    @pl.when(i % 2 == 1)
    def _():
        score_phase(1, i - 1, False)
        value_phase(0, pending)
        value_phase(1, i - 1)

    @pl.when(i % 2 == 0)
    def _():
        value_phase(0, pending)
    outs = []
    for hd in range(N_HEADS):
        base = hd * VT_ROWS
        outs.append(acc_ref[base:base + V_HEAD, :] / acc_ref[base + V_HEAD:base + V_HEAD + 1, :])
    o_ref[...] = jnp.concatenate(outs, axis=0).T.astype(BF16)


def _flash(qt, k, vt, batch, seq):
    assert seq % ATT_T == 0 and ATT_T % CHUNK == 0
    nq = seq // ATT_T
    k3 = k.reshape(batch, seq, SLAB)
    return pl.pallas_call(
        _flash_kernel,
        grid=(batch, nq),
        in_specs=[
            pl.BlockSpec((None, SLAB, ATT_T), lambda b, i: (b * nq + i, 0, 0)),
            pl.BlockSpec((None, seq, SLAB), lambda b, i: (b, 0, 0)),
            pl.BlockSpec((nq, N_HEADS * VT_ROWS, ATT_T), lambda b, i: (b, 0, 0)),
        ],
        out_specs=pl.BlockSpec((ATT_T, ATTN_WIDTH), lambda b, i: (b * nq + i, 0)),
        out_shape=jax.ShapeDtypeStruct((batch * seq, ATTN_WIDTH), BF16),
        scratch_shapes=[pltpu.VMEM((N_HEADS, ATT_T), F32), pltpu.VMEM((N_HEADS * VT_ROWS, ATT_T), F32),
                        pltpu.VMEM((2, N_HEADS, ATT_T, ATT_T), F32), pltpu.VMEM((2, N_HEADS, ATT_T), F32)],
        compiler_params=pltpu.CompilerParams(dimension_semantics=("parallel", "parallel"),
                                             vmem_limit_bytes=VMEM_LIMIT),
        name="flash",
    )(qt, k3, vt)


def _cache_attn_kernel(q_ref, cache_ref, ckr_ref, cnew_ref, krnew_ref, gk_ref, wc_ref, wuk_ref, e_ref,
                       wuv_ref, o_ref, s_ref, *, past, t_new, tk):
    n_tiles = past // tk
    ncol = N_HEADS * t_new
    gk = gk_ref[...]
    qt, qr = [], []
    for hd in range(N_HEADS):
        qg = q_ref[:, hd * LANE:(hd + 1) * LANE].astype(F32) * gk
        qt.append(jnp.dot(qg.astype(BF16), wc_ref[hd], preferred_element_type=F32))
        qr.append(qg[:, :QK_ROPE])
    qt = jnp.concatenate(qt, axis=0).astype(BF16)
    qr = jnp.concatenate(qr, axis=0).astype(BF16)

    def scores(c, kr):
        cb = c.astype(BF16)
        kn = jnp.dot(cb, wuk_ref[...], preferred_element_type=F32)
        nsq = kn * kn
        hi = nsq.astype(BF16)
        lo = (nsq - hi.astype(F32)).astype(BF16)
        e = e_ref[...]
        ssq = jnp.dot(hi, e, preferred_element_type=F32) + jnp.dot(lo, e, preferred_element_type=F32)
        ssq = ssq + jnp.sum(kr * kr, axis=-1, keepdims=True)
        inv = lax.rsqrt(ssq * (1.0 / QK_HEAD) + EPS)
        nt = (((1,), (1,)), ((), ()))
        s = lax.dot_general(cb, qt, nt, preferred_element_type=F32)
        s = s + lax.dot_general(kr.astype(BF16), qr, nt, preferred_element_type=F32)
        return s * inv

    def pass1(j, m):
        start = pl.multiple_of(j * tk, tk)
        s = scores(cache_ref[pl.ds(start, tk), :], ckr_ref[pl.ds(start, tk), :])
        s_ref[pl.ds(start, tk), :] = s
        return jnp.maximum(m, jnp.max(s, axis=0, keepdims=True))

    m = lax.fori_loop(0, n_tiles, pass1, jnp.full((1, ncol), NEG, F32))
    c_new = cnew_ref[...]
    s_new = scores(c_new, krnew_ref[...])
    m = jnp.maximum(m, jnp.max(s_new, axis=0, keepdims=True))

    tn = (((0,), (0,)), ((), ()))

    def accumulate(carry, s, c):
        l, ctx = carry
        p = jnp.exp2(s - m)
        l = l + jnp.sum(p, axis=0, keepdims=True)
        ctx = ctx + lax.dot_general(p.astype(BF16), c.astype(BF16), tn, preferred_element_type=F32)
        return l, ctx

    def pass2(j, carry):
        start = pl.multiple_of(j * tk, tk)
        return accumulate(carry, s_ref[pl.ds(start, tk), :], cache_ref[pl.ds(start, tk), :])

    carry = (jnp.zeros((1, ncol), F32), jnp.zeros((ncol, KV_LORA), F32))
    carry = lax.fori_loop(0, n_tiles, pass2, carry)
    l, ctx = accumulate(carry, s_new, c_new)

    eye = lax.broadcasted_iota(jnp.int32, (ncol, ncol), 0) == lax.broadcasted_iota(jnp.int32, (ncol, ncol), 1)
    l_col = jnp.sum(jnp.where(eye, jnp.broadcast_to(l, (ncol, ncol)), 0.0), axis=1, keepdims=True)
    ctx = (ctx / l_col).astype(BF16)
    r = jnp.dot(ctx, wuv_ref[...], preferred_element_type=F32)
    col_head = lax.broadcasted_iota(jnp.int32, (t_new, ATTN_WIDTH), 1) // V_HEAD
    out = jnp.zeros((t_new, ATTN_WIDTH), F32)
    for hd in range(N_HEADS):
        out = out + jnp.where(col_head == hd, r[hd * t_new:(hd + 1) * t_new, :], 0.0)
    o_ref[...] = out.astype(BF16)


def _cache_attn(q, cache_ckv, cache_krope, layer, c_new, kr_new, w, nseq, t_new, tk):
    past = cache_ckv.shape[2]
    assert past % tk == 0
    ncol = N_HEADS * t_new
    return pl.pallas_call(
        functools.partial(_cache_attn_kernel, past=past, t_new=t_new, tk=tk),
        grid=(nseq,),
        in_specs=[
            pl.BlockSpec((t_new, SLAB), lambda b: (b, 0)),
            pl.BlockSpec((None, None, past, KV_LORA), lambda b: (layer, b, 0, 0)),
            pl.BlockSpec((None, None, past, QK_ROPE), lambda b: (layer, b, 0, 0)),
            pl.BlockSpec((t_new, KV_LORA), lambda b: (b, 0)),
            pl.BlockSpec((t_new, QK_ROPE), lambda b: (b, 0)),
        ] + [_const_spec(a.shape) for a in (w["gk1"], w["wc"], w["wuk"], w["e"], w["wv"])],
        out_specs=pl.BlockSpec((t_new, ATTN_WIDTH), lambda b: (b, 0)),
        out_shape=jax.ShapeDtypeStruct((nseq * t_new, ATTN_WIDTH), BF16),
        scratch_shapes=[pltpu.VMEM((past, ncol), F32)],
        compiler_params=pltpu.CompilerParams(dimension_semantics=("parallel",),
                                             vmem_limit_bytes=VMEM_LIMIT),
        name="cache_attn",
    )(q, cache_ckv, cache_krope, c_new, kr_new, w["gk1"], w["wc"], w["wuk"], w["e"], w["wv"])


def _merge_kernel(*refs, groups, rows, tiles_per_seq, pos0, has_hist):
    if has_hist:
        x_ref, a_ref, hist_ref = refs[:3]
        refs = refs[3:]
    else:
        x_ref, a_ref = refs[:2]
        hist_ref = None
        refs = refs[2:]
    (gmix_ref, w2_ref, wpool_ref, pscale_ref, wao_ref, wpo_ref, wo_ref, gmlp_ref, wup_ref, wdown_ref,
     y_ref, pool_ref, ext_ref) = refs
    tm = groups * rows
    tile = pl.program_id(0) % tiles_per_seq

    x = x_ref[...]
    h = _rms(x, gmix_ref[...]).astype(BF16)
    z = jnp.dot(h, w2_ref[...], preferred_element_type=F32)
    d = x.shape[1]
    p = z[:, :POOL_WIDTH]
    p3 = p.reshape(groups, rows, POOL_WIDTH)

    if has_hist:
        ext_ref[:, :POOL_PAD, :] = hist_ref[...]
    else:
        @pl.when(tile == 0)
        def _():
            ext_ref[:, :POOL_PAD, :] = jnp.zeros((groups, POOL_PAD, POOL_WIDTH), F32)

        @pl.when(tile != 0)
        def _():
            ext_ref[:, :POOL_PAD, :] = ext_ref[:, rows:rows + POOL_PAD, :]
    ext_ref[:, POOL_PAD:, :] = p3
    pool_ref[...] = p3[:, rows - POOL_PAD:, :]

    pos = pos0 + tile * rows + lax.broadcasted_iota(jnp.int32, (groups, rows, POOL_GROUP), 1)
    us = []
    for g, win in enumerate(POOL_WINDOWS):
        sl = slice(g * POOL_GROUP, (g + 1) * POOL_GROUP)
        tot = p3[:, :, sl]
        for j in range(1, win):
            tot = tot + ext_ref[:, pl.ds(POOL_PAD - j, rows), sl]
        cnt = jnp.minimum(win, pos + 1).astype(F32)
        pooled = (tot / cnt - p3[:, :, sl]).reshape(tm, POOL_GROUP)
        u = jnp.dot(pooled.astype(BF16), wpool_ref[g], preferred_element_type=F32)
        us.append((u * pscale_ref[:, sl]).astype(BF16))
    u = jnp.concatenate(us, axis=1)

    branch_a = jnp.dot(a_ref[...], wao_ref[...], preferred_element_type=F32)
    branch_b = jnp.dot(u, wpo_ref[...], preferred_element_type=F32)
    gate_a = 1.0 / (1.0 + jnp.exp(-z[:, POOL_WIDTH:POOL_WIDTH + d]))
    gate_b = 1.0 / (1.0 + jnp.exp(-z[:, POOL_WIDTH + d:]))
    mix = (gate_a * branch_a + gate_b * branch_b).astype(BF16)
    x = x + jnp.dot(mix, wo_ref[...], preferred_element_type=F32)
    hm = _rms(x, gmlp_ref[...]).astype(BF16)
    up = jnp.dot(hm, wup_ref[...], preferred_element_type=F32)
    act = jnp.square(jnp.maximum(up, 0.0)).astype(BF16)
    y_ref[...] = x + jnp.dot(act, wdown_ref[...], preferred_element_type=F32)


def _merge(x, attn, hist, w, nseq, seq, tm, pos0):
    t, d = x.shape
    if hist is None:
        assert seq % tm == 0
        groups, rows, tiles_per_seq = 1, tm, seq // tm
    else:
        assert tm % seq == 0 and seq >= POOL_PAD
        groups, rows, tiles_per_seq = tm // seq, seq, 1
    assert rows % 8 == 0 and rows >= POOL_PAD
    row = lambda i: (i, 0)
    seq_block = lambda i: (i // tiles_per_seq, 0, 0)
    in_specs = [pl.BlockSpec((tm, d), row), pl.BlockSpec((tm, ATTN_WIDTH), row)]
    args = [x, attn]
    if hist is not None:
        in_specs.append(pl.BlockSpec((groups, POOL_PAD, POOL_WIDTH), seq_block))
        args.append(hist)
    weights = (w["g_mix"], w["w2"], w["w_pool"], w["pool_scale"], w["w_attn_out"], w["w_pool_out"],
               w["w_o"], w["g_mlp"], w["w_up"], w["w_down"])
    in_specs += [_const_spec(a.shape) for a in weights]
    return pl.pallas_call(
        functools.partial(_merge_kernel, groups=groups, rows=rows, tiles_per_seq=tiles_per_seq,
                          pos0=pos0, has_hist=hist is not None),
        grid=(t // tm,),
        in_specs=in_specs,
        out_specs=(pl.BlockSpec((tm, d), row),
                   pl.BlockSpec((groups, POOL_PAD, POOL_WIDTH), seq_block)),
        out_shape=(jax.ShapeDtypeStruct((t, d), F32),
                   jax.ShapeDtypeStruct((nseq, POOL_PAD, POOL_WIDTH), F32)),
        scratch_shapes=[pltpu.VMEM((groups, POOL_PAD + rows, POOL_WIDTH), F32)],
        compiler_params=pltpu.CompilerParams(dimension_semantics=("arbitrary",),
                                             vmem_limit_bytes=VMEM_LIMIT),
        name="merge",
    )(*args, *weights)


def _rope_tables(pos):
    half = QK_ROPE // 2
    inv = jnp.power(ROPE_THETA, -jnp.arange(half, dtype=F32) / half)
    ang = pos[:, None] * inv[None, :]
    cos, sin = jnp.cos(ang), jnp.sin(ang)
    n = pos.shape[0]
    one = jnp.ones((n, LANE - QK_ROPE), F32)
    zero = jnp.zeros((n, LANE - half), F32)
    c = jnp.concatenate([cos, cos, one], axis=1)
    s1 = jnp.concatenate([jnp.zeros((n, half), F32), sin, jnp.zeros((n, LANE - QK_ROPE), F32)], axis=1)
    s2 = jnp.concatenate([-sin, zero], axis=1)
    return c, s1, s2


def _slab(rope_part, nope_part):
    pad = jnp.zeros(rope_part.shape[:-1] + (LANE - QK_HEAD,), rope_part.dtype)
    s = jnp.concatenate([rope_part, nope_part, pad], axis=-1)
    return s.reshape(s.shape[:-2] + (N_HEADS * LANE,))


def _layer_weights(l, t_new, g_mix, w_in, g_qa, g_kva, w_uq, w_ukv, g_q, g_k, w_attn_out, w_pool,
                   pool_scale, w_pool_out, w_o, g_mlp, w_up, w_down):
    off_kv, off_kr = Q_LORA, Q_LORA + KV_LORA
    off_p = off_kr + QK_ROPE
    wi = w_in[l]
    d = wi.shape[0]
    w1 = jnp.concatenate([wi[:, :off_p], jnp.zeros((d, LANE - QK_ROPE), F32)], axis=1)
    uq = w_uq[l]
    ukv = w_ukv[l]
    k_nope, v = ukv[..., :QK_NOPE], ukv[..., QK_NOPE:]
    zr = jnp.zeros(k_nope.shape[:-1] + (QK_ROPE,), F32)
    gq, gk = g_q[l], g_k[l]
    slab_gain = lambda g: jnp.concatenate([g[QK_NOPE:], g[:QK_NOPE], jnp.zeros((LANE - QK_HEAD,), F32)])[None]
    wc = jnp.zeros((N_HEADS, LANE, KV_LORA), F32).at[:, QK_ROPE:QK_HEAD, :].set(k_nope.transpose(1, 2, 0))
    ncol = N_HEADS * t_new
    e = (jnp.arange(N_HEADS * QK_NOPE)[:, None] // QK_NOPE == jnp.arange(ncol)[None, :] // t_new)
    return {
        "g_mix": g_mix[l][None],
        "w1": w1.astype(BF16),
        "w2": wi[:, off_p:].astype(BF16),
        "g_qa": g_qa[l][None],
        "g_kva": g_kva[l][None],
        "wq": _slab(uq[..., QK_NOPE:], uq[..., :QK_NOPE]).astype(BF16),
        "wk": _slab(zr, k_nope).astype(BF16),
        "wv": v.reshape(KV_LORA, ATTN_WIDTH).astype(BF16),
        "wuk": k_nope.reshape(KV_LORA, N_HEADS * QK_NOPE).astype(BF16),
        "wc": wc.astype(BF16),
        "e": e.astype(BF16),
        "gq": slab_gain(gq) * (SM_SCALE * LOG2E),
        "gk": slab_gain(gk),
        "gk1": slab_gain(gk),
        "w_attn_out": w_attn_out[l].astype(BF16),
        "w_pool": w_pool[l].astype(BF16),
        "pool_scale": pool_scale[l][None],
        "w_pool_out": w_pool_out[l].astype(BF16),
        "w_o": w_o[l].astype(BF16),
        "g_mlp": g_mlp[l][None],
        "w_up": w_up[l].astype(BF16),
        "w_down": w_down[l].astype(BF16),
    }


def kernel(x_prompt, x_sample, cache_ckv, cache_krope, state_pool, g_mix, w_in, g_qa, g_kva, w_uq, w_ukv,
           g_q, g_k, w_attn_out, w_pool, pool_scale, w_pool_out, w_o, g_mlp, w_up, w_down):
    batch, seq, d = x_prompt.shape
    nseq, t_new, _ = x_sample.shape
    depth = g_mix.shape[0]
    past = cache_ckv.shape[2]

    tabs_p = _rope_tables(jnp.arange(seq, dtype=F32))
    proj_tm_s = min(PROJ_TM, nseq * t_new)
    tabs_s = tuple(jnp.tile(a, (proj_tm_s // t_new, 1))
                   for a in _rope_tables(past + jnp.arange(t_new, dtype=F32)))
    hist = jnp.pad(state_pool, ((0, 0), (0, 0), (POOL_PAD - POOL_HIST, 0), (0, 0)))

    xp = x_prompt.reshape(batch * seq, d)
    xs = x_sample.reshape(nseq * t_new, d)
    outs = {k: [] for k in ("ckv_p", "kr_p", "pool_p", "ckv_s", "kr_s", "pool_s")}
    for l in range(depth):
        w = _layer_weights(l, t_new, g_mix, w_in, g_qa, g_kva, w_uq, w_ukv, g_q, g_k, w_attn_out, w_pool,
                           pool_scale, w_pool_out, w_o, g_mlp, w_up, w_down)
        qt, k, vt, ckv, kr = _proj(xp, tabs_p, seq // PROJ_TM, w, PROJ_TM, True)
        attn = _flash(qt, k, vt, batch, seq)
        xp, pool = _merge(xp, attn, None, w, batch, seq, MERGE_TM, 0)
        outs["ckv_p"].append(ckv.reshape(batch, seq, KV_LORA))
        outs["kr_p"].append(kr.reshape(batch, seq, QK_ROPE))
        outs["pool_p"].append(pool[:, POOL_PAD - POOL_HIST:])

        q, ckv, kr = _proj(xs, tabs_s, 1, w, proj_tm_s, False)
        attn = _cache_attn(q, cache_ckv, cache_krope, l, ckv, kr, w, nseq, t_new, CACHE_TK)
        xs, pool = _merge(xs, attn, hist[l], w, nseq, t_new, MERGE_TM, past)
        outs["ckv_s"].append(ckv.reshape(nseq, t_new, KV_LORA))
        outs["kr_s"].append(kr.reshape(nseq, t_new, QK_ROPE))
        outs["pool_s"].append(pool[:, POOL_PAD - POOL_HIST:])

    return (xp.reshape(batch, seq, d), xs.reshape(nseq, t_new, d),
            jnp.stack(outs["ckv_p"]), jnp.stack(outs["kr_p"]), jnp.stack(outs["pool_p"]),
            jnp.stack(outs["ckv_s"]), jnp.stack(outs["kr_s"]), jnp.stack(outs["pool_s"]))
```

```python
import functools

import jax
import jax.numpy as jnp
from jax import lax
from jax.experimental import pallas as pl
from jax.experimental.pallas import tpu as pltpu

F32 = jnp.float32
BF16 = jnp.bfloat16

LANE = 128
CHUNK = 64
N_HEADS = 8
QK_NOPE = 64
QK_ROPE = 32
QK_HEAD = QK_NOPE + QK_ROPE
V_HEAD = 64
Q_LORA = 384
KV_LORA = 256
ATTN_WIDTH = N_HEADS * V_HEAD
VT_ROWS = V_HEAD + 16
POOL_WINDOWS = (2, 4, 8, 16)
POOL_GROUP = 128
POOL_WIDTH = len(POOL_WINDOWS) * POOL_GROUP
POOL_HIST = max(POOL_WINDOWS) - 1
POOL_PAD = POOL_HIST + 1
ROPE_THETA = 10000.0
EPS = 1e-6
SM_SCALE = QK_HEAD ** -0.5
LOG2E = 1.4426950408889634
SLAB = N_HEADS * LANE
NEG = -1e30

PROJ_TM = 512
ATT_T = 256
MERGE_TM = 256
CACHE_TK = 512
VMEM_LIMIT = 56 * 1024 * 1024


def _const_spec(shape):
    nd = len(shape)
    return pl.BlockSpec(shape, lambda *_: (0,) * nd, pipeline_mode=pl.Buffered(1))


def _rms(x, g):
    ms = jnp.mean(x * x, axis=-1, keepdims=True)
    return x * lax.rsqrt(ms + EPS) * g


def _rope_slab(x, c, s1, s2):
    return x * c + pltpu.roll(x, 16, 1) * s1 + pltpu.roll(x, LANE - 16, 1) * s2


def _head_norm(x, g):
    ss = jnp.sum(x * x, axis=-1, keepdims=True)
    return x * lax.rsqrt(ss * (1.0 / QK_HEAD) + EPS) * g


def _proj_kernel(x_ref, c_ref, s1_ref, s2_ref, ct_ref, st_ref, gmix_ref, w1_ref, gqa_ref, gkva_ref, wq_ref,
                 wqt_ref, wk_ref, wvt_ref, gq_ref, gqt_ref, gk_ref, *out_refs, transposed):
    if transposed:
        q_ref, k_ref, v_ref, ckv_ref, kr_ref = out_refs
    else:
        q_ref, ckv_ref, kr_ref = out_refs
    half = QK_ROPE // 2
    for t in range(x_ref.shape[0] // ATT_T):
        rows = slice(t * ATT_T, (t + 1) * ATT_T)
        h = _rms(x_ref[rows, :], gmix_ref[...]).astype(BF16)
        z = jnp.dot(h, w1_ref[...], preferred_element_type=F32)
        cq = _rms(z[:, :Q_LORA], gqa_ref[...]).astype(BF16)
        ckv = _rms(z[:, Q_LORA:Q_LORA + KV_LORA], gkva_ref[...])
        ckv_ref[rows, :] = ckv
        c, s1, s2 = c_ref[rows, :], s1_ref[rows, :], s2_ref[rows, :]
        kr = _rope_slab(z[:, Q_LORA + KV_LORA:], c, s1, s2)
        kr_ref[rows, :] = kr[:, :QK_ROPE]
        if not transposed:
            q = jnp.dot(cq, wq_ref[...], preferred_element_type=F32)
            for hd in range(N_HEADS):
                sl = slice(hd * LANE, (hd + 1) * LANE)
                q_ref[rows, sl] = _head_norm(_rope_slab(q[:, sl], c, s1, s2), gq_ref[...]).astype(BF16)
            continue
        ckv_b = ckv.astype(BF16)
        kn = jnp.dot(ckv_b, wk_ref[...], preferred_element_type=F32)
        nt = (((1,), (1,)), ((), ()))
        vt = lax.dot_general(wvt_ref[...], ckv_b, nt, preferred_element_type=F32).astype(BF16)
        qt_all = lax.dot_general(wqt_ref[...], cq, nt, preferred_element_type=F32)
        ones = jnp.ones((VT_ROWS - V_HEAD, ATT_T), BF16)
        cos, sin = ct_ref[:, rows], st_ref[:, rows]
        for hd in range(N_HEADS):
            sl = slice(hd * LANE, (hd + 1) * LANE)
            v_ref[t, hd * VT_ROWS:hd * VT_ROWS + V_HEAD, :] = vt[hd * V_HEAD:(hd + 1) * V_HEAD, :]
            v_ref[t, hd * VT_ROWS + V_HEAD:(hd + 1) * VT_ROWS, :] = ones
            qt = qt_all[sl, :]
            x1, x2 = qt[:half], qt[half:QK_ROPE]
            qt = jnp.concatenate([x1 * cos - x2 * sin, x1 * sin + x2 * cos, qt[QK_ROPE:]], axis=0)
            ss = jnp.sum(qt * qt, axis=0, keepdims=True)
            q_ref[t, sl, :] = (qt * lax.rsqrt(ss * (1.0 / QK_HEAD) + EPS) * gqt_ref[...]).astype(BF16)
            k_ref[rows, sl] = _head_norm(kn[:, sl] + kr, gk_ref[...]).astype(BF16)


def _proj(x, tabs, tab_tiles, w, tm, transposed):
    t = x.shape[0]
    d = x.shape[1]
    assert t % tm == 0 and tm % ATT_T == 0
    c_tab, s1_tab, s2_tab, cos_t, sin_t = tabs
    row = lambda i: (i, 0)
    lead = lambda i: (i, 0, 0)
    tab = pl.BlockSpec((tm, LANE), lambda i: (i % tab_tiles, 0))
    tab_t = pl.BlockSpec((QK_ROPE // 2, tm), lambda i: (0, i % tab_tiles))
    weights = (w["g_mix"], w["w1"], w["g_qa"], w["g_kva"], w["wq"], w["wq_t"], w["wk"], w["wv_t"], w["gq"],
               w["gq_t"], w["gk"])
    in_specs = [pl.BlockSpec((tm, d), row), tab, tab, tab, tab_t, tab_t] + [
        _const_spec(a.shape) for a in weights]
    latent_shapes = (jax.ShapeDtypeStruct((t, KV_LORA), F32), jax.ShapeDtypeStruct((t, QK_ROPE), F32))
    latent_specs = (pl.BlockSpec((tm, KV_LORA), row), pl.BlockSpec((tm, QK_ROPE), row))
    if transposed:
        nt = tm // ATT_T
        out_shape = (
            jax.ShapeDtypeStruct((t // ATT_T, SLAB, ATT_T), BF16),
            jax.ShapeDtypeStruct((t, SLAB), BF16),
            jax.ShapeDtypeStruct((t // ATT_T, N_HEADS * VT_ROWS, ATT_T), BF16),
        ) + latent_shapes
        out_specs = (
            pl.BlockSpec((nt, SLAB, ATT_T), lead),
            pl.BlockSpec((tm, SLAB), row),
            pl.BlockSpec((nt, N_HEADS * VT_ROWS, ATT_T), lead),
        ) + latent_specs
    else:
        out_shape = (jax.ShapeDtypeStruct((t, SLAB), BF16),) + latent_shapes
        out_specs = (pl.BlockSpec((tm, SLAB), row),) + latent_specs
    return pl.pallas_call(
        functools.partial(_proj_kernel, transposed=transposed),
        grid=(t // tm,),
        in_specs=in_specs,
        out_specs=out_specs,
        out_shape=out_shape,
        compiler_params=pltpu.CompilerParams(dimension_semantics=("parallel",),
                                             vmem_limit_bytes=VMEM_LIMIT),
        name="proj",
    )(x, c_tab, s1_tab, s2_tab, cos_t, sin_t, *weights)


def _flash_kernel(qt_ref, k_ref, vt_ref, o_ref, m_ref, acc_ref, s_ref, mt_ref):
    t = ATT_T
    i = pl.program_id(1)
    m_ref[...] = jnp.full(m_ref.shape, NEG, F32)
    acc_ref[...] = jnp.zeros(acc_ref.shape, F32)

    def score_phase(slot, tile, masked):
        start = pl.multiple_of(tile * t, t)
        for hd in range(N_HEADS):
            qsl = slice(hd * LANE, (hd + 1) * LANE)
            s = jnp.dot(k_ref[pl.ds(start, t), qsl], qt_ref[qsl, :], preferred_element_type=F32)
            if masked:
                k_chunk = lax.broadcasted_iota(jnp.int32, (t, t), 0) // CHUNK
                q_chunk = lax.broadcasted_iota(jnp.int32, (t, t), 1) // CHUNK
                s = jnp.where(k_chunk <= q_chunk, s, NEG)
            s_ref[slot, hd] = s
            mt_ref[slot, hd:hd + 1, :] = jnp.max(s, axis=0, keepdims=True)

    def value_phase(slot, tile):
        for hd in range(N_HEADS):
            vsl = slice(hd * VT_ROWS, (hd + 1) * VT_ROWS)
            m_old = m_ref[hd:hd + 1, :]
            m_new = jnp.maximum(m_old, mt_ref[slot, hd:hd + 1, :])
            alpha = jnp.exp2(m_old - m_new)
            p = jnp.exp2(s_ref[slot, hd] - m_new).astype(BF16)
            pv = jnp.dot(vt_ref[tile, vsl, :], p, preferred_element_type=F32)
            acc_ref[vsl, :] = alpha * acc_ref[vsl, :] + pv
            m_ref[hd:hd + 1, :] = m_new

    score_phase(0, i, True)
    n_pairs = i // 2

    def pair(p, carry):
        score_phase(1, 2 * p, False)
        value_phase(0, jnp.where(p == 0, i, 2 * p - 1))
        score_phase(0, 2 * p + 1, False)
        value_phase(1, 2 * p)
        return carry

    lax.fori_loop(0, n_pairs, pair, 0)
    pending = jnp.where(n_pairs == 0, i, 2 * n_pairs - 1)

    @pl.when(i % 2 == 1)
    def _():
        score_phase(1, i - 1, False)
        value_phase(0, pending)
        value_phase(1, i - 1)

    @pl.when(i % 2 == 0)
    def _():
        value_phase(0, pending)
    outs = []
    for hd in range(N_HEADS):
        base = hd * VT_ROWS
        outs.append(acc_ref[base:base + V_HEAD, :] / acc_ref[base + V_HEAD:base + V_HEAD + 1, :])
    o_ref[...] = jnp.concatenate(outs, axis=0).T.astype(BF16)


def _flash(qt, k, vt, batch, seq):
    assert seq % ATT_T == 0 and ATT_T % CHUNK == 0
    nq = seq // ATT_T
    k3 = k.reshape(batch, seq, SLAB)
    return pl.pallas_call(
        _flash_kernel,
        grid=(batch, nq),
        in_specs=[
            pl.BlockSpec((None, SLAB, ATT_T), lambda b, i: (b * nq + i, 0, 0)),
            pl.BlockSpec((None, seq, SLAB), lambda b, i: (b, 0, 0)),
            pl.BlockSpec((nq, N_HEADS * VT_ROWS, ATT_T), lambda b, i: (b, 0, 0)),
        ],
        out_specs=pl.BlockSpec((ATT_T, ATTN_WIDTH), lambda b, i: (b * nq + i, 0)),
        out_shape=jax.ShapeDtypeStruct((batch * seq, ATTN_WIDTH), BF16),
        scratch_shapes=[pltpu.VMEM((N_HEADS, ATT_T), F32), pltpu.VMEM((N_HEADS * VT_ROWS, ATT_T), F32),
                        pltpu.VMEM((2, N_HEADS, ATT_T, ATT_T), F32), pltpu.VMEM((2, N_HEADS, ATT_T), F32)],
        compiler_params=pltpu.CompilerParams(dimension_semantics=("parallel", "parallel"),
                                             vmem_limit_bytes=VMEM_LIMIT),
        name="flash",
    )(qt, k3, vt)


def _cache_attn_kernel(q_ref, cache_ref, ckr_ref, cnew_ref, krnew_ref, gk_ref, wc_ref, wuk_ref, e_ref,
                       wuv_ref, o_ref, s_ref, *, past, t_new, tk):
    n_tiles = past // tk
    ncol = N_HEADS * t_new
    gk = gk_ref[...]
    qt, qr = [], []
    for hd in range(N_HEADS):
        qg = q_ref[:, hd * LANE:(hd + 1) * LANE].astype(F32) * gk
        qt.append(jnp.dot(qg.astype(BF16), wc_ref[hd], preferred_element_type=F32))
        qr.append(qg[:, :QK_ROPE])
    qt = jnp.concatenate(qt, axis=0).astype(BF16)
    qr = jnp.concatenate(qr, axis=0).astype(BF16)

    def scores(c, kr):
        cb = c.astype(BF16)
        kn = jnp.dot(cb, wuk_ref[...], preferred_element_type=F32)
        ssq = jnp.dot((kn * kn).astype(BF16), e_ref[...], preferred_element_type=F32)
        ssq = ssq + jnp.sum(kr * kr, axis=-1, keepdims=True)
        inv = lax.rsqrt(ssq * (1.0 / QK_HEAD) + EPS)
        nt = (((1,), (1,)), ((), ()))
        s = lax.dot_general(cb, qt, nt, preferred_element_type=F32)
        s = s + lax.dot_general(kr.astype(BF16), qr, nt, preferred_element_type=F32)
        return s * inv

    def pass1(j, m):
        start = pl.multiple_of(j * tk, tk)
        s = scores(cache_ref[pl.ds(start, tk), :], ckr_ref[:, pl.ds(start, tk)].T)
        s_ref[pl.ds(start, tk), :] = s
        return jnp.maximum(m, jnp.max(s, axis=0, keepdims=True))

    m = lax.fori_loop(0, n_tiles, pass1, jnp.full((1, ncol), NEG, F32))
    c_new = cnew_ref[...]
    s_new = scores(c_new, krnew_ref[...])
    m = jnp.maximum(m, jnp.max(s_new, axis=0, keepdims=True))

    tn = (((0,), (0,)), ((), ()))

    def accumulate(carry, s, c):
        l, ctx = carry
        p = jnp.exp2(s - m)
        l = l + jnp.sum(p, axis=0, keepdims=True)
        ctx = ctx + lax.dot_general(p.astype(BF16), c.astype(BF16), tn, preferred_element_type=F32)
        return l, ctx

    def pass2(j, carry):
        start = pl.multiple_of(j * tk, tk)
        return accumulate(carry, s_ref[pl.ds(start, tk), :], cache_ref[pl.ds(start, tk), :])

    carry = (jnp.zeros((1, ncol), F32), jnp.zeros((ncol, KV_LORA), F32))
    carry = lax.fori_loop(0, n_tiles, pass2, carry)
    l, ctx = accumulate(carry, s_new, c_new)

    eye = lax.broadcasted_iota(jnp.int32, (ncol, ncol), 0) == lax.broadcasted_iota(jnp.int32, (ncol, ncol), 1)
    l_col = jnp.sum(jnp.where(eye, jnp.broadcast_to(l, (ncol, ncol)), 0.0), axis=1, keepdims=True)
    ctx = (ctx / l_col).astype(BF16)
    r = jnp.dot(ctx, wuv_ref[...], preferred_element_type=F32)
    col_head = lax.broadcasted_iota(jnp.int32, (t_new, ATTN_WIDTH), 1) // V_HEAD
    out = jnp.zeros((t_new, ATTN_WIDTH), F32)
    for hd in range(N_HEADS):
        out = out + jnp.where(col_head == hd, r[hd * t_new:(hd + 1) * t_new, :], 0.0)
    o_ref[...] = out.astype(BF16)


def _cache_attn(q, cache_ckv, cache_krope_t, layer, c_new, kr_new, w, nseq, t_new, tk):
    past = cache_ckv.shape[2]
    assert past % tk == 0
    ncol = N_HEADS * t_new
    return pl.pallas_call(
        functools.partial(_cache_attn_kernel, past=past, t_new=t_new, tk=tk),
        grid=(nseq,),
        in_specs=[
            pl.BlockSpec((t_new, SLAB), lambda b: (b, 0)),
            pl.BlockSpec((None, None, past, KV_LORA), lambda b: (layer, b, 0, 0)),
            pl.BlockSpec((None, None, QK_ROPE, past), lambda b: (layer, b, 0, 0)),
            pl.BlockSpec((t_new, KV_LORA), lambda b: (b, 0)),
            pl.BlockSpec((t_new, QK_ROPE), lambda b: (b, 0)),
        ] + [_const_spec(a.shape) for a in (w["gk1"], w["wc"], w["wuk"], w["e"], w["wv"])],
        out_specs=pl.BlockSpec((t_new, ATTN_WIDTH), lambda b: (b, 0)),
        out_shape=jax.ShapeDtypeStruct((nseq * t_new, ATTN_WIDTH), BF16),
        scratch_shapes=[pltpu.VMEM((past, ncol), F32)],
        compiler_params=pltpu.CompilerParams(dimension_semantics=("parallel",),
                                             vmem_limit_bytes=VMEM_LIMIT),
        name="cache_attn",
    )(q, cache_ckv, cache_krope_t, c_new, kr_new, w["gk1"], w["wc"], w["wuk"], w["e"], w["wv"])


def _merge_kernel(*refs, groups, rows, tiles_per_seq, pos0, has_hist):
    if has_hist:
        x_ref, a_ref, hist_ref = refs[:3]
        refs = refs[3:]
    else:
        x_ref, a_ref = refs[:2]
        hist_ref = None
        refs = refs[2:]
    (gmix_ref, w2_ref, wpool_ref, pscale_ref, wao_ref, wpo_ref, wo_ref, gmlp_ref, wup_ref, wdown_ref,
     y_ref, pool_ref, ext_ref) = refs
    tm = groups * rows
    tile = pl.program_id(0) % tiles_per_seq

    x = x_ref[...]
    h = _rms(x, gmix_ref[...]).astype(BF16)
    z = jnp.dot(h, w2_ref[...], preferred_element_type=F32)
    d = x.shape[1]
    p = z[:, :POOL_WIDTH]
    p3 = p.reshape(groups, rows, POOL_WIDTH)

    if has_hist:
        ext_ref[:, :POOL_PAD, :] = hist_ref[...]
    else:
        @pl.when(tile == 0)
        def _():
            ext_ref[:, :POOL_PAD, :] = jnp.zeros((groups, POOL_PAD, POOL_WIDTH), F32)

        @pl.when(tile != 0)
        def _():
            ext_ref[:, :POOL_PAD, :] = ext_ref[:, rows:rows + POOL_PAD, :]
    ext_ref[:, POOL_PAD:, :] = p3
    pool_ref[...] = p3[:, rows - POOL_PAD:, :]

    pos = pos0 + tile * rows + lax.broadcasted_iota(jnp.int32, (groups, rows, POOL_GROUP), 1)
    us = []
    for g, win in enumerate(POOL_WINDOWS):
        sl = slice(g * POOL_GROUP, (g + 1) * POOL_GROUP)
        tot = p3[:, :, sl]
        for j in range(1, win):
            tot = tot + ext_ref[:, pl.ds(POOL_PAD - j, rows), sl]
        cnt = jnp.minimum(win, pos + 1).astype(F32)
        pooled = (tot / cnt - p3[:, :, sl]).reshape(tm, POOL_GROUP)
        u = jnp.dot(pooled.astype(BF16), wpool_ref[g], preferred_element_type=F32)
        us.append((u * pscale_ref[:, sl]).astype(BF16))
    u = jnp.concatenate(us, axis=1)

    branch_a = jnp.dot(a_ref[...], wao_ref[...], preferred_element_type=F32)
    branch_b = jnp.dot(u, wpo_ref[...], preferred_element_type=F32)
    gate_a = 1.0 / (1.0 + jnp.exp(-z[:, POOL_WIDTH:POOL_WIDTH + d]))
    gate_b = 1.0 / (1.0 + jnp.exp(-z[:, POOL_WIDTH + d:]))
    mix = (gate_a * branch_a + gate_b * branch_b).astype(BF16)
    x = x + jnp.dot(mix, wo_ref[...], preferred_element_type=F32)
    hm = _rms(x, gmlp_ref[...]).astype(BF16)
    up = jnp.dot(hm, wup_ref[...], preferred_element_type=F32)
    act = jnp.square(jnp.maximum(up, 0.0)).astype(BF16)
    y_ref[...] = x + jnp.dot(act, wdown_ref[...], preferred_element_type=F32)


def _merge(x, attn, hist, w, nseq, seq, tm, pos0):
    t, d = x.shape
    if hist is None:
        assert seq % tm == 0
        groups, rows, tiles_per_seq = 1, tm, seq // tm
    else:
        assert tm % seq == 0 and seq >= POOL_PAD
        groups, rows, tiles_per_seq = tm // seq, seq, 1
    assert rows % 8 == 0 and rows >= POOL_PAD
    row = lambda i: (i, 0)
    seq_block = lambda i: (i // tiles_per_seq, 0, 0)
    in_specs = [pl.BlockSpec((tm, d), row), pl.BlockSpec((tm, ATTN_WIDTH), row)]
    args = [x, attn]
    if hist is not None:
        in_specs.append(pl.BlockSpec((groups, POOL_PAD, POOL_WIDTH), seq_block))
        args.append(hist)
    weights = (w["g_mix"], w["w2"], w["w_pool"], w["pool_scale"], w["w_attn_out"], w["w_pool_out"],
               w["w_o"], w["g_mlp"], w["w_up"], w["w_down"])
    in_specs += [_const_spec(a.shape) for a in weights]
    return pl.pallas_call(
        functools.partial(_merge_kernel, groups=groups, rows=rows, tiles_per_seq=tiles_per_seq,
                          pos0=pos0, has_hist=hist is not None),
        grid=(t // tm,),
        in_specs=in_specs,
        out_specs=(pl.BlockSpec((tm, d), row),
                   pl.BlockSpec((groups, POOL_PAD, POOL_WIDTH), seq_block)),
        out_shape=(jax.ShapeDtypeStruct((t, d), F32),
                   jax.ShapeDtypeStruct((nseq, POOL_PAD, POOL_WIDTH), F32)),
        scratch_shapes=[pltpu.VMEM((groups, POOL_PAD + rows, POOL_WIDTH), F32)],
        compiler_params=pltpu.CompilerParams(dimension_semantics=("arbitrary",),
                                             vmem_limit_bytes=VMEM_LIMIT),
        name="merge",
    )(*args, *weights)


def _rope_tables(pos):
    half = QK_ROPE // 2
    inv = jnp.power(ROPE_THETA, -jnp.arange(half, dtype=F32) / half)
    ang = pos[:, None] * inv[None, :]
    cos, sin = jnp.cos(ang), jnp.sin(ang)
    n = pos.shape[0]
    one = jnp.ones((n, LANE - QK_ROPE), F32)
    zero = jnp.zeros((n, LANE - half), F32)
    c = jnp.concatenate([cos, cos, one], axis=1)
    s1 = jnp.concatenate([jnp.zeros((n, half), F32), sin, jnp.zeros((n, LANE - QK_ROPE), F32)], axis=1)
    s2 = jnp.concatenate([-sin, zero], axis=1)
    return c, s1, s2, cos.T, sin.T


def _slab(rope_part, nope_part):
    pad = jnp.zeros(rope_part.shape[:-1] + (LANE - QK_HEAD,), rope_part.dtype)
    s = jnp.concatenate([rope_part, nope_part, pad], axis=-1)
    return s.reshape(s.shape[:-2] + (N_HEADS * LANE,))


def _layer_weights(l, t_new, g_mix, w_in, g_qa, g_kva, w_uq, w_ukv, g_q, g_k, w_attn_out, w_pool,
                   pool_scale, w_pool_out, w_o, g_mlp, w_up, w_down):
    off_kv, off_kr = Q_LORA, Q_LORA + KV_LORA
    off_p = off_kr + QK_ROPE
    wi = w_in[l]
    d = wi.shape[0]
    w1 = jnp.concatenate([wi[:, :off_p], jnp.zeros((d, LANE - QK_ROPE), F32)], axis=1)
    uq = w_uq[l]
    ukv = w_ukv[l]
    k_nope, v = ukv[..., :QK_NOPE], ukv[..., QK_NOPE:]
    zr = jnp.zeros(k_nope.shape[:-1] + (QK_ROPE,), F32)
    gq, gk = g_q[l], g_k[l]
    slab_gain = lambda g: jnp.concatenate([g[QK_NOPE:], g[:QK_NOPE], jnp.zeros((LANE - QK_HEAD,), F32)])[None]
    wc = jnp.zeros((N_HEADS, LANE, KV_LORA), F32).at[:, QK_ROPE:QK_HEAD, :].set(k_nope.transpose(1, 2, 0))
    ncol = N_HEADS * t_new
    e = (jnp.arange(N_HEADS * QK_NOPE)[:, None] // QK_NOPE == jnp.arange(ncol)[None, :] // t_new)
    return {
        "g_mix": g_mix[l][None],
        "w1": w1.astype(BF16),
        "w2": wi[:, off_p:].astype(BF16),
        "g_qa": g_qa[l][None],
        "g_kva": g_kva[l][None],
        "wq": _slab(uq[..., QK_NOPE:], uq[..., :QK_NOPE]).astype(BF16),
        "wq_t": _slab(uq[..., QK_NOPE:], uq[..., :QK_NOPE]).T.astype(BF16),
        "wk": _slab(zr, k_nope).astype(BF16),
        "wv": v.reshape(KV_LORA, ATTN_WIDTH).astype(BF16),
        "wv_t": v.reshape(KV_LORA, ATTN_WIDTH).T.astype(BF16),
        "wuk": k_nope.reshape(KV_LORA, N_HEADS * QK_NOPE).astype(BF16),
        "wc": wc.astype(BF16),
        "e": e.astype(BF16),
        "gq": slab_gain(gq) * (SM_SCALE * LOG2E),
        "gq_t": slab_gain(gq).T * (SM_SCALE * LOG2E),
        "gk": slab_gain(gk),
        "gk1": slab_gain(gk),
        "w_attn_out": w_attn_out[l].astype(BF16),
        "w_pool": w_pool[l].astype(BF16),
        "pool_scale": pool_scale[l][None],
        "w_pool_out": w_pool_out[l].astype(BF16),
        "w_o": w_o[l].astype(BF16),
        "g_mlp": g_mlp[l][None],
        "w_up": w_up[l].astype(BF16),
        "w_down": w_down[l].astype(BF16),
    }


def kernel(x_prompt, x_sample, cache_ckv, cache_krope, state_pool, g_mix, w_in, g_qa, g_kva, w_uq, w_ukv,
           g_q, g_k, w_attn_out, w_pool, pool_scale, w_pool_out, w_o, g_mlp, w_up, w_down):
    batch, seq, d = x_prompt.shape
    nseq, t_new, _ = x_sample.shape
    depth = g_mix.shape[0]
    past = cache_ckv.shape[2]

    tabs_p = _rope_tables(jnp.arange(seq, dtype=F32))
    proj_tm_s = min(PROJ_TM, nseq * t_new)
    reps = proj_tm_s // t_new
    tabs_s = _rope_tables(past + jnp.arange(t_new, dtype=F32))
    tabs_s = tuple(jnp.tile(a, (reps, 1)) for a in tabs_s[:3]) + tuple(jnp.tile(a, (1, reps)) for a in tabs_s[3:])
    hist = jnp.pad(state_pool, ((0, 0), (0, 0), (POOL_PAD - POOL_HIST, 0), (0, 0)))
    krope_t = jnp.swapaxes(cache_krope, 2, 3)

    xp = x_prompt.reshape(batch * seq, d)
    xs = x_sample.reshape(nseq * t_new, d)
    outs = {k: [] for k in ("ckv_p", "kr_p", "pool_p", "ckv_s", "kr_s", "pool_s")}
    for l in range(depth):
        w = _layer_weights(l, t_new, g_mix, w_in, g_qa, g_kva, w_uq, w_ukv, g_q, g_k, w_attn_out, w_pool,
                           pool_scale, w_pool_out, w_o, g_mlp, w_up, w_down)
        qt, k, vt, ckv, kr = _proj(xp, tabs_p, seq // PROJ_TM, w, PROJ_TM, True)
        attn = _flash(qt, k, vt, batch, seq)
        xp, pool = _merge(xp, attn, None, w, batch, seq, MERGE_TM, 0)
        outs["ckv_p"].append(ckv.reshape(batch, seq, KV_LORA))
        outs["kr_p"].append(kr.reshape(batch, seq, QK_ROPE))
        outs["pool_p"].append(pool[:, POOL_PAD - POOL_HIST:])

        q, ckv, kr = _proj(xs, tabs_s, 1, w, proj_tm_s, False)
        attn = _cache_attn(q, cache_ckv, krope_t, l, ckv, kr, w, nseq, t_new, CACHE_TK)
        xs, pool = _merge(xs, attn, hist[l], w, nseq, t_new, MERGE_TM, past)
        outs["ckv_s"].append(ckv.reshape(nseq, t_new, KV_LORA))
        outs["kr_s"].append(kr.reshape(nseq, t_new, QK_ROPE))
        outs["pool_s"].append(pool[:, POOL_PAD - POOL_HIST:])

    return (xp.reshape(batch, seq, d), xs.reshape(nseq, t_new, d),
            jnp.stack(outs["ckv_p"]), jnp.stack(outs["kr_p"]), jnp.stack(outs["pool_p"]),
            jnp.stack(outs["ckv_s"]), jnp.stack(outs["kr_s"]), jnp.stack(outs["pool_s"]))
```

```python
import functools

import jax
import jax.numpy as jnp
from jax import lax
from jax.experimental import pallas as pl
from jax.experimental.pallas import tpu as pltpu

F32 = jnp.float32
BF16 = jnp.bfloat16

LANE = 128
CHUNK = 64
N_HEADS = 8
QK_NOPE = 64
QK_ROPE = 32
QK_HEAD = QK_NOPE + QK_ROPE
V_HEAD = 64
Q_LORA = 384
KV_LORA = 256
ATTN_WIDTH = N_HEADS * V_HEAD
VT_ROWS = V_HEAD + 16
POOL_WINDOWS = (2, 4, 8, 16)
POOL_GROUP = 128
POOL_WIDTH = len(POOL_WINDOWS) * POOL_GROUP
POOL_HIST = max(POOL_WINDOWS) - 1
POOL_PAD = POOL_HIST + 1
ROPE_THETA = 10000.0
EPS = 1e-6
SM_SCALE = QK_HEAD ** -0.5
LOG2E = 1.4426950408889634
SLAB = N_HEADS * LANE
NEG = -1e30

PROJ_TM = 512
ATT_T = 256
MERGE_TM = 512
CACHE_TK = 512
VMEM_LIMIT = 56 * 1024 * 1024


def _const_spec(shape):
    nd = len(shape)
    return pl.BlockSpec(shape, lambda *_: (0,) * nd, pipeline_mode=pl.Buffered(1))


def _rms(x, g):
    ms = jnp.mean(x * x, axis=-1, keepdims=True)
    return x * lax.rsqrt(ms + EPS) * g


def _rope_slab(x, c, s1, s2):
    return x * c + pltpu.roll(x, 16, 1) * s1 + pltpu.roll(x, LANE - 16, 1) * s2


def _head_norm(x, g):
    ss = jnp.sum(x * x, axis=-1, keepdims=True)
    return x * lax.rsqrt(ss * (1.0 / QK_HEAD) + EPS) * g


def _proj_kernel(x_ref, c_ref, s1_ref, s2_ref, ct_ref, st_ref, gmix_ref, w1_ref, gqa_ref, gkva_ref, wq_ref,
                 wqt_ref, wk_ref, wvt_ref, gq_ref, gqt_ref, gk_ref, *out_refs, transposed, aliased):
    if aliased:
        out_refs = out_refs[1:]
    if transposed:
        q_ref, k_ref, v_ref, ckv_ref, kr_ref = out_refs
    else:
        q_ref, ckv_ref, kr_ref = out_refs
    half = QK_ROPE // 2
    for t in range(x_ref.shape[0] // ATT_T):
        rows = slice(t * ATT_T, (t + 1) * ATT_T)
        h = _rms(x_ref[rows, :], gmix_ref[...]).astype(BF16)
        z = jnp.dot(h, w1_ref[...], preferred_element_type=F32)
        cq = _rms(z[:, :Q_LORA], gqa_ref[...]).astype(BF16)
        ckv = _rms(z[:, Q_LORA:Q_LORA + KV_LORA], gkva_ref[...])
        ckv_ref[rows, :] = ckv
        c, s1, s2 = c_ref[rows, :], s1_ref[rows, :], s2_ref[rows, :]
        kr = _rope_slab(z[:, Q_LORA + KV_LORA:], c, s1, s2)
        kr_ref[rows, :] = kr[:, :QK_ROPE]
        if not transposed:
            q = jnp.dot(cq, wq_ref[...], preferred_element_type=F32)
            for hd in range(N_HEADS):
                sl = slice(hd * LANE, (hd + 1) * LANE)
                q_ref[rows, sl] = _head_norm(_rope_slab(q[:, sl], c, s1, s2), gq_ref[...]).astype(BF16)
            continue
        ckv_b = ckv.astype(BF16)
        kn = jnp.dot(ckv_b, wk_ref[...], preferred_element_type=F32)
        nt = (((1,), (1,)), ((), ()))
        vt = lax.dot_general(wvt_ref[...], ckv_b, nt, preferred_element_type=F32).astype(BF16)
        qt_all = lax.dot_general(wqt_ref[...], cq, nt, preferred_element_type=F32)
        ones = jnp.ones((VT_ROWS - V_HEAD, ATT_T), BF16)
        cos, sin = ct_ref[:, rows], st_ref[:, rows]
        for hd in range(N_HEADS):
            sl = slice(hd * LANE, (hd + 1) * LANE)
            v_ref[t, hd * VT_ROWS:hd * VT_ROWS + V_HEAD, :] = vt[hd * V_HEAD:(hd + 1) * V_HEAD, :]
            v_ref[t, hd * VT_ROWS + V_HEAD:(hd + 1) * VT_ROWS, :] = ones
            qt = qt_all[sl, :]
            x1, x2 = qt[:half], qt[half:QK_ROPE]
            qt = jnp.concatenate([x1 * cos - x2 * sin, x1 * sin + x2 * cos, qt[QK_ROPE:]], axis=0)
            ss = jnp.sum(qt * qt, axis=0, keepdims=True)
            q_ref[t, sl, :] = (qt * lax.rsqrt(ss * (1.0 / QK_HEAD) + EPS) * gqt_ref[...]).astype(BF16)
            k_ref[rows, sl] = _head_norm(kn[:, sl] + kr, gk_ref[...]).astype(BF16)


def _proj(x, tabs, tab_tiles, w, tm, transposed, layer, depth, ckv_all):
    t = x.shape[0]
    d = x.shape[1]
    assert t % tm == 0 and tm % ATT_T == 0
    c_tab, s1_tab, s2_tab, cos_t, sin_t = tabs
    row = lambda i: (i, 0)
    lead = lambda i: (i, 0, 0)
    tab = pl.BlockSpec((tm, LANE), lambda i: (i % tab_tiles, 0))
    tab_t = pl.BlockSpec((QK_ROPE // 2, tm), lambda i: (0, i % tab_tiles))
    weights = (w["g_mix"], w["w1"], w["g_qa"], w["g_kva"], w["wq"], w["wq_t"], w["wk"], w["wv_t"], w["gq"],
               w["gq_t"], w["gk"])
    in_specs = [pl.BlockSpec((tm, d), row), tab, tab, tab, tab_t, tab_t] + [
        _const_spec(a.shape) for a in weights]
    latent_shapes = (jax.ShapeDtypeStruct((depth, t, KV_LORA), F32), jax.ShapeDtypeStruct((t, QK_ROPE), F32))
    latent_specs = (pl.BlockSpec((None, tm, KV_LORA), lambda i: (layer, i, 0)), pl.BlockSpec((tm, QK_ROPE), row))
    if transposed:
        nt = tm // ATT_T
        out_shape = (
            jax.ShapeDtypeStruct((t // ATT_T, SLAB, ATT_T), BF16),
            jax.ShapeDtypeStruct((t, SLAB), BF16),
            jax.ShapeDtypeStruct((t // ATT_T, N_HEADS * VT_ROWS, ATT_T), BF16),
        ) + latent_shapes
        out_specs = (
            pl.BlockSpec((nt, SLAB, ATT_T), lead),
            pl.BlockSpec((tm, SLAB), row),
            pl.BlockSpec((nt, N_HEADS * VT_ROWS, ATT_T), lead),
        ) + latent_specs
    else:
        out_shape = (jax.ShapeDtypeStruct((t, SLAB), BF16),) + latent_shapes
        out_specs = (pl.BlockSpec((tm, SLAB), row),) + latent_specs
    args = [x, c_tab, s1_tab, s2_tab, cos_t, sin_t, *weights]
    aliases = {}
    if ckv_all is not None:
        in_specs.append(pl.BlockSpec(memory_space=pl.ANY))
        aliases = {len(args): len(out_shape) - 2}
        args.append(ckv_all)
    return pl.pallas_call(
        functools.partial(_proj_kernel, transposed=transposed, aliased=ckv_all is not None),
        grid=(t // tm,),
        in_specs=in_specs,
        out_specs=out_specs,
        out_shape=out_shape,
        input_output_aliases=aliases,
        compiler_params=pltpu.CompilerParams(dimension_semantics=("parallel",),
                                             vmem_limit_bytes=VMEM_LIMIT),
        name="proj",
    )(*args)


def _flash_kernel(qt_ref, k_ref, vt_ref, o_ref, m_ref, acc_ref, s_ref, mt_ref):
    t = ATT_T
    i = pl.program_id(1)
    m_ref[...] = jnp.full(m_ref.shape, NEG, F32)
    acc_ref[...] = jnp.zeros(acc_ref.shape, F32)

    def score_phase(slot, tile, masked):
        start = pl.multiple_of(tile * t, t)
        for hd in range(N_HEADS):
            qsl = slice(hd * LANE, (hd + 1) * LANE)
            s = jnp.dot(k_ref[pl.ds(start, t), qsl], qt_ref[qsl, :], preferred_element_type=F32)
            if masked:
                k_chunk = lax.broadcasted_iota(jnp.int32, (t, t), 0) // CHUNK
                q_chunk = lax.broadcasted_iota(jnp.int32, (t, t), 1) // CHUNK
                s = jnp.where(k_chunk <= q_chunk, s, NEG)
            s_ref[slot, hd] = s
            mt_ref[slot, hd:hd + 1, :] = jnp.max(s, axis=0, keepdims=True)

    def value_phase(slot, tile):
        for hd in range(N_HEADS):
            vsl = slice(hd * VT_ROWS, (hd + 1) * VT_ROWS)
            m_old = m_ref[hd:hd + 1, :]
            m_new = jnp.maximum(m_old, mt_ref[slot, hd:hd + 1, :])
            alpha = jnp.exp2(m_old - m_new)
            p = jnp.exp2(s_ref[slot, hd] - m_new).astype(BF16)
            pv = jnp.dot(vt_ref[tile, vsl, :], p, preferred_element_type=F32)
            acc_ref[vsl, :] = alpha * acc_ref[vsl, :] + pv
            m_ref[hd:hd + 1, :] = m_new

    score_phase(0, i, True)
    n_pairs = i // 2

    def pair(p, carry):
        score_phase(1, 2 * p, False)
        value_phase(0, jnp.where(p == 0, i, 2 * p - 1))
        score_phase(0, 2 * p + 1, False)
        value_phase(1, 2 * p)
        return carry

    lax.fori_loop(0, n_pairs, pair, 0)
    pending = jnp.where(n_pairs == 0, i, 2 * n_pairs - 1)

    @pl.when(i % 2 == 1)
    def _():
        score_phase(1, i - 1, False)
        value_phase(0, pending)
        value_phase(1, i - 1)

    @pl.when(i % 2 == 0)
    def _():
        value_phase(0, pending)
    outs = []
    for hd in range(N_HEADS):
        base = hd * VT_ROWS
        outs.append(acc_ref[base:base + V_HEAD, :] / acc_ref[base + V_HEAD:base + V_HEAD + 1, :])
    o_ref[...] = jnp.concatenate(outs, axis=0).T.astype(BF16)


def _flash(qt, k, vt, batch, seq):
    assert seq % ATT_T == 0 and ATT_T % CHUNK == 0
    nq = seq // ATT_T
    k3 = k.reshape(batch, seq, SLAB)
    return pl.pallas_call(
        _flash_kernel,
        grid=(batch, nq),
        in_specs=[
            pl.BlockSpec((None, SLAB, ATT_T), lambda b, i: (b * nq + i, 0, 0)),
            pl.BlockSpec((None, seq, SLAB), lambda b, i: (b, 0, 0)),
            pl.BlockSpec((nq, N_HEADS * VT_ROWS, ATT_T), lambda b, i: (b, 0, 0)),
        ],
        out_specs=pl.BlockSpec((ATT_T, ATTN_WIDTH), lambda b, i: (b * nq + i, 0)),
        out_shape=jax.ShapeDtypeStruct((batch * seq, ATTN_WIDTH), BF16),
        scratch_shapes=[pltpu.VMEM((N_HEADS, ATT_T), F32), pltpu.VMEM((N_HEADS * VT_ROWS, ATT_T), F32),
                        pltpu.VMEM((2, N_HEADS, ATT_T, ATT_T), F32), pltpu.VMEM((2, N_HEADS, ATT_T), F32)],
        compiler_params=pltpu.CompilerParams(dimension_semantics=("parallel", "parallel"),
                                             vmem_limit_bytes=VMEM_LIMIT),
        name="flash",
    )(qt, k3, vt)


def _cache_attn_kernel(q_ref, cache_ref, ckr_ref, cnew_ref, krnew_ref, gk_ref, wc_ref, wuk_ref, e_ref,
                       wuv_ref, o_ref, s_ref, *, past, t_new, tk):
    n_tiles = past // tk
    ncol = N_HEADS * t_new
    gk = gk_ref[...]
    qt, qr = [], []
    for hd in range(N_HEADS):
        qg = q_ref[:, hd * LANE:(hd + 1) * LANE].astype(F32) * gk
        qt.append(jnp.dot(qg.astype(BF16), wc_ref[hd], preferred_element_type=F32))
        qr.append(qg[:, :QK_ROPE])
    qt = jnp.concatenate(qt, axis=0).astype(BF16)
    qr = jnp.concatenate(qr, axis=0).astype(BF16)

    def scores(c, kr):
        cb = c.astype(BF16)
        kn = jnp.dot(cb, wuk_ref[...], preferred_element_type=F32)
        ssq = jnp.dot((kn * kn).astype(BF16), e_ref[...], preferred_element_type=F32)
        ssq = ssq + jnp.sum(kr * kr, axis=-1, keepdims=True)
        inv = lax.rsqrt(ssq * (1.0 / QK_HEAD) + EPS)
        nt = (((1,), (1,)), ((), ()))
        s = lax.dot_general(cb, qt, nt, preferred_element_type=F32)
        s = s + lax.dot_general(kr.astype(BF16), qr, nt, preferred_element_type=F32)
        return s * inv

    def pass1(j, m):
        start = pl.multiple_of(j * tk, tk)
        s = scores(cache_ref[pl.ds(start, tk), :], ckr_ref[:, pl.ds(start, tk)].T)
        s_ref[pl.ds(start, tk), :] = s
        return jnp.maximum(m, jnp.max(s, axis=0, keepdims=True))

    m = lax.fori_loop(0, n_tiles, pass1, jnp.full((1, ncol), NEG, F32))
    c_new = cnew_ref[...]
    s_new = scores(c_new, krnew_ref[...])
    m = jnp.maximum(m, jnp.max(s_new, axis=0, keepdims=True))

    tn = (((0,), (0,)), ((), ()))

    def accumulate(carry, s, c):
        l, ctx = carry
        p = jnp.exp2(s - m)
        l = l + jnp.sum(p, axis=0, keepdims=True)
        ctx = ctx + lax.dot_general(p.astype(BF16), c.astype(BF16), tn, preferred_element_type=F32)
        return l, ctx

    def pass2(j, carry):
        start = pl.multiple_of(j * tk, tk)
        return accumulate(carry, s_ref[pl.ds(start, tk), :], cache_ref[pl.ds(start, tk), :])

    carry = (jnp.zeros((1, ncol), F32), jnp.zeros((ncol, KV_LORA), F32))
    carry = lax.fori_loop(0, n_tiles, pass2, carry)
    l, ctx = accumulate(carry, s_new, c_new)

    eye = lax.broadcasted_iota(jnp.int32, (ncol, ncol), 0) == lax.broadcasted_iota(jnp.int32, (ncol, ncol), 1)
    l_col = jnp.sum(jnp.where(eye, jnp.broadcast_to(l, (ncol, ncol)), 0.0), axis=1, keepdims=True)
    ctx = (ctx / l_col).astype(BF16)
    r = jnp.dot(ctx, wuv_ref[...], preferred_element_type=F32)
    col_head = lax.broadcasted_iota(jnp.int32, (t_new, ATTN_WIDTH), 1) // V_HEAD
    out = jnp.zeros((t_new, ATTN_WIDTH), F32)
    for hd in range(N_HEADS):
        out = out + jnp.where(col_head == hd, r[hd * t_new:(hd + 1) * t_new, :], 0.0)
    o_ref[...] = out.astype(BF16)


def _cache_attn(q, cache_ckv, cache_krope_t, layer, c_new, kr_new, w, nseq, t_new, tk):
    past = cache_ckv.shape[2]
    assert past % tk == 0
    ncol = N_HEADS * t_new
    return pl.pallas_call(
        functools.partial(_cache_attn_kernel, past=past, t_new=t_new, tk=tk),
        grid=(nseq,),
        in_specs=[
            pl.BlockSpec((t_new, SLAB), lambda b: (b, 0)),
            pl.BlockSpec((None, None, past, KV_LORA), lambda b: (layer, b, 0, 0)),
            pl.BlockSpec((None, None, QK_ROPE, past), lambda b: (layer, b, 0, 0)),
            pl.BlockSpec((t_new, KV_LORA), lambda b: (b, 0)),
            pl.BlockSpec((t_new, QK_ROPE), lambda b: (b, 0)),
        ] + [_const_spec(a.shape) for a in (w["gk1"], w["wc"], w["wuk"], w["e"], w["wv"])],
        out_specs=pl.BlockSpec((t_new, ATTN_WIDTH), lambda b: (b, 0)),
        out_shape=jax.ShapeDtypeStruct((nseq * t_new, ATTN_WIDTH), BF16),
        scratch_shapes=[pltpu.VMEM((past, ncol), F32)],
        compiler_params=pltpu.CompilerParams(dimension_semantics=("parallel",),
                                             vmem_limit_bytes=VMEM_LIMIT),
        name="cache_attn",
    )(q, cache_ckv, cache_krope_t, c_new, kr_new, w["gk1"], w["wc"], w["wuk"], w["e"], w["wv"])


def _merge_kernel(*refs, groups, rows, tiles_per_seq, pos0, has_hist):
    if has_hist:
        x_ref, a_ref, hist_ref = refs[:3]
        refs = refs[3:]
    else:
        x_ref, a_ref = refs[:2]
        hist_ref = None
        refs = refs[2:]
    (gmix_ref, w2_ref, wpool_ref, pscale_ref, wao_ref, wpo_ref, wo_ref, gmlp_ref, wup_ref, wdown_ref,
     y_ref, pool_ref, ext_ref) = refs
    tm = groups * rows
    tile = pl.program_id(0) % tiles_per_seq

    x = x_ref[...]
    h = _rms(x, gmix_ref[...]).astype(BF16)
    z = jnp.dot(h, w2_ref[...], preferred_element_type=F32)
    d = x.shape[1]
    p = z[:, :POOL_WIDTH]
    p3 = p.reshape(groups, rows, POOL_WIDTH)

    if has_hist:
        ext_ref[:, :POOL_PAD, :] = hist_ref[...]
    else:
        @pl.when(tile == 0)
        def _():
            ext_ref[:, :POOL_PAD, :] = jnp.zeros((groups, POOL_PAD, POOL_WIDTH), F32)

        @pl.when(tile != 0)
        def _():
            ext_ref[:, :POOL_PAD, :] = ext_ref[:, rows:rows + POOL_PAD, :]
    ext_ref[:, POOL_PAD:, :] = p3
    pool_ref[...] = p3[:, rows - POOL_PAD:, :]

    pos = pos0 + tile * rows + lax.broadcasted_iota(jnp.int32, (groups, rows, POOL_GROUP), 1)
    us = []
    for g, win in enumerate(POOL_WINDOWS):
        sl = slice(g * POOL_GROUP, (g + 1) * POOL_GROUP)
        tot = p3[:, :, sl]
        for j in range(1, win):
            tot = tot + ext_ref[:, pl.ds(POOL_PAD - j, rows), sl]
        cnt = jnp.minimum(win, pos + 1).astype(F32)
        pooled = (tot / cnt - p3[:, :, sl]).reshape(tm, POOL_GROUP)
        u = jnp.dot(pooled.astype(BF16), wpool_ref[g], preferred_element_type=F32)
        us.append((u * pscale_ref[:, sl]).astype(BF16))
    u = jnp.concatenate(us, axis=1)

    branch_a = jnp.dot(a_ref[...], wao_ref[...], preferred_element_type=F32)
    branch_b = jnp.dot(u, wpo_ref[...], preferred_element_type=F32)
    gate_a = 1.0 / (1.0 + jnp.exp(-z[:, POOL_WIDTH:POOL_WIDTH + d]))
    gate_b = 1.0 / (1.0 + jnp.exp(-z[:, POOL_WIDTH + d:]))
    mix = (gate_a * branch_a + gate_b * branch_b).astype(BF16)
    x = x + jnp.dot(mix, wo_ref[...], preferred_element_type=F32)
    hm = _rms(x, gmlp_ref[...]).astype(BF16)
    up = jnp.dot(hm, wup_ref[...], preferred_element_type=F32)
    act = jnp.square(jnp.maximum(up, 0.0)).astype(BF16)
    y_ref[...] = x + jnp.dot(act, wdown_ref[...], preferred_element_type=F32)


def _merge(x, attn, hist, w, nseq, seq, tm, pos0):
    t, d = x.shape
    assert t % tm == 0
    if hist is None:
        assert seq % tm == 0
        groups, rows, tiles_per_seq = 1, tm, seq // tm
    else:
        assert tm % seq == 0 and seq >= POOL_PAD
        groups, rows, tiles_per_seq = tm // seq, seq, 1
    assert rows % 8 == 0 and rows >= POOL_PAD
    row = lambda i: (i, 0)
    seq_block = lambda i: (i // tiles_per_seq, 0, 0)
    in_specs = [pl.BlockSpec((tm, d), row), pl.BlockSpec((tm, ATTN_WIDTH), row)]
    args = [x, attn]
    if hist is not None:
        in_specs.append(pl.BlockSpec((groups, POOL_PAD, POOL_WIDTH), seq_block))
        args.append(hist)
    weights = (w["g_mix"], w["w2"], w["w_pool"], w["pool_scale"], w["w_attn_out"], w["w_pool_out"],
               w["w_o"], w["g_mlp"], w["w_up"], w["w_down"])
    in_specs += [_const_spec(a.shape) for a in weights]
    return pl.pallas_call(
        functools.partial(_merge_kernel, groups=groups, rows=rows, tiles_per_seq=tiles_per_seq,
                          pos0=pos0, has_hist=hist is not None),
        grid=(t // tm,),
        in_specs=in_specs,
        out_specs=(pl.BlockSpec((tm, d), row),
                   pl.BlockSpec((groups, POOL_PAD, POOL_WIDTH), seq_block)),
        out_shape=(jax.ShapeDtypeStruct((t, d), F32),
                   jax.ShapeDtypeStruct((nseq, POOL_PAD, POOL_WIDTH), F32)),
        scratch_shapes=[pltpu.VMEM((groups, POOL_PAD + rows, POOL_WIDTH), F32)],
        compiler_params=pltpu.CompilerParams(dimension_semantics=("arbitrary",),
                                             vmem_limit_bytes=VMEM_LIMIT),
        name="merge",
    )(*args, *weights)


def _rope_tables(pos):
    half = QK_ROPE // 2
    inv = jnp.power(ROPE_THETA, -jnp.arange(half, dtype=F32) / half)
    ang = pos[:, None] * inv[None, :]
    cos, sin = jnp.cos(ang), jnp.sin(ang)
    n = pos.shape[0]
    one = jnp.ones((n, LANE - QK_ROPE), F32)
    zero = jnp.zeros((n, LANE - half), F32)
    c = jnp.concatenate([cos, cos, one], axis=1)
    s1 = jnp.concatenate([jnp.zeros((n, half), F32), sin, jnp.zeros((n, LANE - QK_ROPE), F32)], axis=1)
    s2 = jnp.concatenate([-sin, zero], axis=1)
    return c, s1, s2, cos.T, sin.T


def _slab(rope_part, nope_part):
    pad = jnp.zeros(rope_part.shape[:-1] + (LANE - QK_HEAD,), rope_part.dtype)
    s = jnp.concatenate([rope_part, nope_part, pad], axis=-1)
    return s.reshape(s.shape[:-2] + (N_HEADS * LANE,))


def _layer_weights(l, t_new, g_mix, w_in, g_qa, g_kva, w_uq, w_ukv, g_q, g_k, w_attn_out, w_pool,
                   pool_scale, w_pool_out, w_o, g_mlp, w_up, w_down):
    off_kv, off_kr = Q_LORA, Q_LORA + KV_LORA
    off_p = off_kr + QK_ROPE
    wi = w_in[l]
    d = wi.shape[0]
    w1 = jnp.concatenate([wi[:, :off_p], jnp.zeros((d, LANE - QK_ROPE), F32)], axis=1)
    uq = w_uq[l]
    ukv = w_ukv[l]
    k_nope, v = ukv[..., :QK_NOPE], ukv[..., QK_NOPE:]
    zr = jnp.zeros(k_nope.shape[:-1] + (QK_ROPE,), F32)
    gq, gk = g_q[l], g_k[l]
    slab_gain = lambda g: jnp.concatenate([g[QK_NOPE:], g[:QK_NOPE], jnp.zeros((LANE - QK_HEAD,), F32)])[None]
    wc = jnp.zeros((N_HEADS, LANE, KV_LORA), F32).at[:, QK_ROPE:QK_HEAD, :].set(k_nope.transpose(1, 2, 0))
    ncol = N_HEADS * t_new
    e = (jnp.arange(N_HEADS * QK_NOPE)[:, None] // QK_NOPE == jnp.arange(ncol)[None, :] // t_new)
    return {
        "g_mix": g_mix[l][None],
        "w1": w1.astype(BF16),
        "w2": wi[:, off_p:].astype(BF16),
        "g_qa": g_qa[l][None],
        "g_kva": g_kva[l][None],
        "wq": _slab(uq[..., QK_NOPE:], uq[..., :QK_NOPE]).astype(BF16),
        "wq_t": _slab(uq[..., QK_NOPE:], uq[..., :QK_NOPE]).T.astype(BF16),
        "wk": _slab(zr, k_nope).astype(BF16),
        "wv": v.reshape(KV_LORA, ATTN_WIDTH).astype(BF16),
        "wv_t": v.reshape(KV_LORA, ATTN_WIDTH).T.astype(BF16),
        "wuk": k_nope.reshape(KV_LORA, N_HEADS * QK_NOPE).astype(BF16),
        "wc": wc.astype(BF16),
        "e": e.astype(BF16),
        "gq": slab_gain(gq) * (SM_SCALE * LOG2E),
        "gq_t": slab_gain(gq).T * (SM_SCALE * LOG2E),
        "gk": slab_gain(gk),
        "gk1": slab_gain(gk),
        "w_attn_out": w_attn_out[l].astype(BF16),
        "w_pool": w_pool[l].astype(BF16),
        "pool_scale": pool_scale[l][None],
        "w_pool_out": w_pool_out[l].astype(BF16),
        "w_o": w_o[l].astype(BF16),
        "g_mlp": g_mlp[l][None],
        "w_up": w_up[l].astype(BF16),
        "w_down": w_down[l].astype(BF16),
    }


def kernel(x_prompt, x_sample, cache_ckv, cache_krope, state_pool, g_mix, w_in, g_qa, g_kva, w_uq, w_ukv,
           g_q, g_k, w_attn_out, w_pool, pool_scale, w_pool_out, w_o, g_mlp, w_up, w_down):
    batch, seq, d = x_prompt.shape
    nseq, t_new, _ = x_sample.shape
    depth = g_mix.shape[0]
    past = cache_ckv.shape[2]

    tabs_p = _rope_tables(jnp.arange(seq, dtype=F32))
    proj_tm_s = min(PROJ_TM, nseq * t_new)
    reps = proj_tm_s // t_new
    tabs_s = _rope_tables(past + jnp.arange(t_new, dtype=F32))
    tabs_s = tuple(jnp.tile(a, (reps, 1)) for a in tabs_s[:3]) + tuple(jnp.tile(a, (1, reps)) for a in tabs_s[3:])
    hist = jnp.pad(state_pool, ((0, 0), (0, 0), (POOL_PAD - POOL_HIST, 0), (0, 0)))
    krope_t = jnp.swapaxes(cache_krope, 2, 3)

    xp = x_prompt.reshape(batch * seq, d)
    xs = x_sample.reshape(nseq * t_new, d)
    outs = {k: [] for k in ("kr_p", "pool_p", "kr_s", "pool_s")}
    ckv_p = ckv_s = None
    for l in range(depth):
        w = _layer_weights(l, t_new, g_mix, w_in, g_qa, g_kva, w_uq, w_ukv, g_q, g_k, w_attn_out, w_pool,
                           pool_scale, w_pool_out, w_o, g_mlp, w_up, w_down)
        qt, k, vt, ckv_p, kr = _proj(xp, tabs_p, seq // PROJ_TM, w, PROJ_TM, True, l, depth, ckv_p)
        attn = _flash(qt, k, vt, batch, seq)
        xp, pool = _merge(xp, attn, None, w, batch, seq, MERGE_TM, 0)
        outs["kr_p"].append(kr.reshape(batch, seq, QK_ROPE))
        outs["pool_p"].append(pool[:, POOL_PAD - POOL_HIST:])

        q, ckv_s, kr = _proj(xs, tabs_s, 1, w, proj_tm_s, False, l, depth, ckv_s)
        attn = _cache_attn(q, cache_ckv, krope_t, l, ckv_s[l], kr, w, nseq, t_new, CACHE_TK)
        xs, pool = _merge(xs, attn, hist[l], w, nseq, t_new, min(MERGE_TM, nseq * t_new), past)
        outs["kr_s"].append(kr.reshape(nseq, t_new, QK_ROPE))
        outs["pool_s"].append(pool[:, POOL_PAD - POOL_HIST:])

    return (xp.reshape(batch, seq, d), xs.reshape(nseq, t_new, d),
            ckv_p.reshape(depth, batch, seq, KV_LORA), jnp.stack(outs["kr_p"]), jnp.stack(outs["pool_p"]),
            ckv_s.reshape(depth, nseq, t_new, KV_LORA), jnp.stack(outs["kr_s"]), jnp.stack(outs["pool_s"]))
```

```python
import functools

import jax
import jax.numpy as jnp
from jax import lax
from jax.experimental import pallas as pl
from jax.experimental.pallas import tpu as pltpu

F32 = jnp.float32
BF16 = jnp.bfloat16

LANE = 128
CHUNK = 64
N_HEADS = 8
QK_NOPE = 64
QK_ROPE = 32
QK_HEAD = QK_NOPE + QK_ROPE
V_HEAD = 64
Q_LORA = 384
KV_LORA = 256
ATTN_WIDTH = N_HEADS * V_HEAD
VT_ROWS = V_HEAD + 16
POOL_WINDOWS = (2, 4, 8, 16)
POOL_GROUP = 128
POOL_WIDTH = len(POOL_WINDOWS) * POOL_GROUP
POOL_HIST = max(POOL_WINDOWS) - 1
POOL_PAD = POOL_HIST + 1
ROPE_THETA = 10000.0
EPS = 1e-6
SM_SCALE = QK_HEAD ** -0.5
LOG2E = 1.4426950408889634
SLAB = N_HEADS * LANE
NEG = -1e30

PROJ_TM = 512
ATT_T = 256
MERGE_TM = 512
CACHE_TK = 512
VMEM_LIMIT = 56 * 1024 * 1024


def _const_spec(shape):
    nd = len(shape)
    return pl.BlockSpec(shape, lambda *_: (0,) * nd, pipeline_mode=pl.Buffered(1))


def _rms(x, g):
    ms = jnp.mean(x * x, axis=-1, keepdims=True)
    return x * lax.rsqrt(ms + EPS) * g


def _rope_slab(x, c, s1, s2):
    return x * c + pltpu.roll(x, 16, 1) * s1 + pltpu.roll(x, LANE - 16, 1) * s2


def _head_norm(x, g):
    ss = jnp.sum(x * x, axis=-1, keepdims=True)
    return x * lax.rsqrt(ss * (1.0 / QK_HEAD) + EPS) * g


def _proj_kernel(x_ref, c_ref, s1_ref, s2_ref, ct_ref, st_ref, gmix_ref, w1_ref, gqa_ref, gkva_ref, wq_ref,
                 wqt_ref, wk_ref, wvt_ref, gq_ref, gqt_ref, gk_ref, *out_refs, transposed, aliased):
    if aliased:
        out_refs = out_refs[1:]
    if transposed:
        q_ref, k_ref, v_ref, ckv_ref, kr_ref = out_refs
    else:
        q_ref, ckv_ref, kr_ref = out_refs
    half = QK_ROPE // 2
    for t in range(x_ref.shape[0] // ATT_T):
        rows = slice(t * ATT_T, (t + 1) * ATT_T)
        h = _rms(x_ref[rows, :], gmix_ref[...]).astype(BF16)
        z = jnp.dot(h, w1_ref[...], preferred_element_type=F32)
        cq = _rms(z[:, :Q_LORA], gqa_ref[...]).astype(BF16)
        ckv = _rms(z[:, Q_LORA:Q_LORA + KV_LORA], gkva_ref[...])
        ckv_ref[rows, :] = ckv
        c, s1, s2 = c_ref[rows, :], s1_ref[rows, :], s2_ref[rows, :]
        kr = _rope_slab(z[:, Q_LORA + KV_LORA:], c, s1, s2)
        kr_ref[rows, :] = kr[:, :QK_ROPE]
        if not transposed:
            q = jnp.dot(cq, wq_ref[...], preferred_element_type=F32)
            for hd in range(N_HEADS):
                sl = slice(hd * LANE, (hd + 1) * LANE)
                q_ref[rows, sl] = _head_norm(_rope_slab(q[:, sl], c, s1, s2), gq_ref[...]).astype(BF16)
            continue
        ckv_b = ckv.astype(BF16)
        kn = jnp.dot(ckv_b, wk_ref[...], preferred_element_type=F32)
        nt = (((1,), (1,)), ((), ()))
        vt = lax.dot_general(wvt_ref[...], ckv_b, nt, preferred_element_type=F32).astype(BF16)
        qt_all = lax.dot_general(wqt_ref[...], cq, nt, preferred_element_type=F32)
        ones = jnp.ones((VT_ROWS - V_HEAD, ATT_T), BF16)
        cos, sin = ct_ref[:, rows], st_ref[:, rows]
        for hd in range(N_HEADS):
            sl = slice(hd * LANE, (hd + 1) * LANE)
            v_ref[t, hd * VT_ROWS:hd * VT_ROWS + V_HEAD, :] = vt[hd * V_HEAD:(hd + 1) * V_HEAD, :]
            v_ref[t, hd * VT_ROWS + V_HEAD:(hd + 1) * VT_ROWS, :] = ones
            qt = qt_all[sl, :]
            x1, x2 = qt[:half], qt[half:QK_ROPE]
            qt = jnp.concatenate([x1 * cos - x2 * sin, x1 * sin + x2 * cos, qt[QK_ROPE:]], axis=0)
            ss = jnp.sum(qt * qt, axis=0, keepdims=True)
            q_ref[t, sl, :] = (qt * lax.rsqrt(ss * (1.0 / QK_HEAD) + EPS) * gqt_ref[...]).astype(BF16)
            k_ref[rows, sl] = _head_norm(kn[:, sl] + kr, gk_ref[...]).astype(BF16)


def _proj(x, tabs, tab_tiles, w, tm, transposed, layer, depth, ckv_all):
    t = x.shape[0]
    d = x.shape[1]
    assert t % tm == 0 and tm % ATT_T == 0
    c_tab, s1_tab, s2_tab, cos_t, sin_t = tabs
    row = lambda i: (i, 0)
    lead = lambda i: (i, 0, 0)
    tab = pl.BlockSpec((tm, LANE), lambda i: (i % tab_tiles, 0))
    tab_t = pl.BlockSpec((QK_ROPE // 2, tm), lambda i: (0, i % tab_tiles))
    weights = (w["g_mix"], w["w1"], w["g_qa"], w["g_kva"], w["wq"], w["wq_t"], w["wk"], w["wv_t"], w["gq"],
               w["gq_t"], w["gk"])
    in_specs = [pl.BlockSpec((tm, d), row), tab, tab, tab, tab_t, tab_t] + [
        _const_spec(a.shape) for a in weights]
    latent_shapes = (jax.ShapeDtypeStruct((depth, t, KV_LORA), F32), jax.ShapeDtypeStruct((t, QK_ROPE), F32))
    latent_specs = (pl.BlockSpec((None, tm, KV_LORA), lambda i: (layer, i, 0)), pl.BlockSpec((tm, QK_ROPE), row))
    if transposed:
        nt = tm // ATT_T
        out_shape = (
            jax.ShapeDtypeStruct((t // ATT_T, SLAB, ATT_T), BF16),
            jax.ShapeDtypeStruct((t, SLAB), BF16),
            jax.ShapeDtypeStruct((t // ATT_T, N_HEADS * VT_ROWS, ATT_T), BF16),
        ) + latent_shapes
        out_specs = (
            pl.BlockSpec((nt, SLAB, ATT_T), lead),
            pl.BlockSpec((tm, SLAB), row),
            pl.BlockSpec((nt, N_HEADS * VT_ROWS, ATT_T), lead),
        ) + latent_specs
    else:
        out_shape = (jax.ShapeDtypeStruct((t, SLAB), BF16),) + latent_shapes
        out_specs = (pl.BlockSpec((tm, SLAB), row),) + latent_specs
    args = [x, c_tab, s1_tab, s2_tab, cos_t, sin_t, *weights]
    aliases = {}
    if ckv_all is not None:
        in_specs.append(pl.BlockSpec(memory_space=pl.ANY))
        aliases = {len(args): len(out_shape) - 2}
        args.append(ckv_all)
    return pl.pallas_call(
        functools.partial(_proj_kernel, transposed=transposed, aliased=ckv_all is not None),
        grid=(t // tm,),
        in_specs=in_specs,
        out_specs=out_specs,
        out_shape=out_shape,
        input_output_aliases=aliases,
        compiler_params=pltpu.CompilerParams(dimension_semantics=("parallel",),
                                             vmem_limit_bytes=VMEM_LIMIT),
        name="proj",
    )(*args)


def _flash_kernel(qt_ref, k_ref, vt_ref, o_ref, m_ref, acc_ref, s_ref, mt_ref):
    t = ATT_T
    i = pl.program_id(1)
    m_ref[...] = jnp.full(m_ref.shape, NEG, F32)
    acc_ref[...] = jnp.zeros(acc_ref.shape, F32)

    def score_head(slot, tile, masked, hd):
        start = pl.multiple_of(tile * t, t)
        qsl = slice(hd * LANE, (hd + 1) * LANE)
        s = jnp.dot(k_ref[pl.ds(start, t), qsl], qt_ref[qsl, :], preferred_element_type=F32)
        if masked:
            k_chunk = lax.broadcasted_iota(jnp.int32, (t, t), 0) // CHUNK
            q_chunk = lax.broadcasted_iota(jnp.int32, (t, t), 1) // CHUNK
            s = jnp.where(k_chunk <= q_chunk, s, NEG)
        s_ref[slot, hd] = s
        mt_ref[slot, hd:hd + 1, :] = jnp.max(s, axis=0, keepdims=True)

    def value_head(slot, tile, hd):
        vsl = slice(hd * VT_ROWS, (hd + 1) * VT_ROWS)
        m_old = m_ref[hd:hd + 1, :]
        m_new = jnp.maximum(m_old, mt_ref[slot, hd:hd + 1, :])
        alpha = jnp.exp2(m_old - m_new)
        p = jnp.exp2(s_ref[slot, hd] - m_new).astype(BF16)
        pv = jnp.dot(vt_ref[tile, vsl, :], p, preferred_element_type=F32)
        acc_ref[vsl, :] = alpha * acc_ref[vsl, :] + pv
        m_ref[hd:hd + 1, :] = m_new

    def score_phase(slot, tile, masked):
        for hd in range(N_HEADS):
            score_head(slot, tile, masked, hd)

    def value_phase(slot, tile):
        for hd in range(N_HEADS):
            value_head(slot, tile, hd)

    def both_phases(score_slot, score_tile, value_slot, value_tile):
        for hd in range(N_HEADS):
            score_head(score_slot, score_tile, False, hd)
            value_head(value_slot, value_tile, hd)

    def tile_pair(first, pending):
        both_phases(1, first, 0, pending)
        both_phases(0, first + 1, 1, first)

    score_phase(0, i, True)
    n_quads = i // 4

    def quad(p, carry):
        tile_pair(4 * p, jnp.where(p == 0, i, 4 * p - 1))
        tile_pair(4 * p + 2, 4 * p + 1)
        return carry

    lax.fori_loop(0, n_quads, quad, 0)
    done = 4 * n_quads
    pending = jnp.where(n_quads == 0, i, done - 1)
    has_pair = i - done >= 2

    @pl.when(has_pair)
    def _():
        tile_pair(done, pending)

    done = jnp.where(has_pair, done + 2, done)
    pending = jnp.where(has_pair, done - 1, pending)

    @pl.when(i > done)
    def _():
        both_phases(1, done, 0, pending)
        value_phase(1, done)

    @pl.when(i == done)
    def _():
        value_phase(0, pending)
    outs = []
    for hd in range(N_HEADS):
        base = hd * VT_ROWS
        outs.append(acc_ref[base:base + V_HEAD, :] / acc_ref[base + V_HEAD:base + V_HEAD + 1, :])
    o_ref[...] = jnp.concatenate(outs, axis=0).T.astype(BF16)


def _flash(qt, k, vt, batch, seq):
    assert seq % ATT_T == 0 and ATT_T % CHUNK == 0
    nq = seq // ATT_T
    k3 = k.reshape(batch, seq, SLAB)
    return pl.pallas_call(
        _flash_kernel,
        grid=(batch, nq),
        in_specs=[
            pl.BlockSpec((None, SLAB, ATT_T), lambda b, i: (b * nq + i, 0, 0)),
            pl.BlockSpec((None, seq, SLAB), lambda b, i: (b, 0, 0)),
            pl.BlockSpec((nq, N_HEADS * VT_ROWS, ATT_T), lambda b, i: (b, 0, 0)),
        ],
        out_specs=pl.BlockSpec((ATT_T, ATTN_WIDTH), lambda b, i: (b * nq + i, 0)),
        out_shape=jax.ShapeDtypeStruct((batch * seq, ATTN_WIDTH), BF16),
        scratch_shapes=[pltpu.VMEM((N_HEADS, ATT_T), F32), pltpu.VMEM((N_HEADS * VT_ROWS, ATT_T), F32),
                        pltpu.VMEM((2, N_HEADS, ATT_T, ATT_T), F32), pltpu.VMEM((2, N_HEADS, ATT_T), F32)],
        compiler_params=pltpu.CompilerParams(dimension_semantics=("parallel", "parallel"),
                                             vmem_limit_bytes=VMEM_LIMIT),
        name="flash",
    )(qt, k3, vt)


def _cache_attn_kernel(q_ref, cache_ref, ckr_ref, cnew_ref, krnew_ref, gk_ref, wc_ref, wuk_ref, e_ref,
                       wuv_ref, o_ref, s_ref, *, past, t_new, tk):
    n_tiles = past // tk
    ncol = N_HEADS * t_new
    gk = gk_ref[...]
    qt, qr = [], []
    for hd in range(N_HEADS):
        qg = q_ref[:, hd * LANE:(hd + 1) * LANE].astype(F32) * gk
        qt.append(jnp.dot(qg.astype(BF16), wc_ref[hd], preferred_element_type=F32))
        qr.append(qg[:, :QK_ROPE])
    qt = jnp.concatenate(qt, axis=0).astype(BF16)
    qr = jnp.concatenate(qr, axis=0).astype(BF16)

    def scores(c, kr):
        cb = c.astype(BF16)
        kn = jnp.dot(cb, wuk_ref[...], preferred_element_type=F32)
        ssq = jnp.dot((kn * kn).astype(BF16), e_ref[...], preferred_element_type=F32)
        ssq = ssq + jnp.sum(kr * kr, axis=-1, keepdims=True)
        inv = lax.rsqrt(ssq * (1.0 / QK_HEAD) + EPS)
        nt = (((1,), (1,)), ((), ()))
        s = lax.dot_general(cb, qt, nt, preferred_element_type=F32)
        s = s + lax.dot_general(kr.astype(BF16), qr, nt, preferred_element_type=F32)
        return s * inv

    def pass1(j, m):
        start = pl.multiple_of(j * tk, tk)
        s = scores(cache_ref[pl.ds(start, tk), :], ckr_ref[:, pl.ds(start, tk)].T)
        s_ref[pl.ds(start, tk), :] = s
        return jnp.maximum(m, jnp.max(s, axis=0, keepdims=True))

    m = lax.fori_loop(0, n_tiles, pass1, jnp.full((1, ncol), NEG, F32))
    c_new = cnew_ref[...]
    s_new = scores(c_new, krnew_ref[...])
    m = jnp.maximum(m, jnp.max(s_new, axis=0, keepdims=True))

    tn = (((0,), (0,)), ((), ()))

    def accumulate(carry, s, c):
        l, ctx = carry
        p = jnp.exp2(s - m)
        l = l + jnp.sum(p, axis=0, keepdims=True)
        ctx = ctx + lax.dot_general(p.astype(BF16), c.astype(BF16), tn, preferred_element_type=F32)
        return l, ctx

    def pass2(j, carry):
        start = pl.multiple_of(j * tk, tk)
        return accumulate(carry, s_ref[pl.ds(start, tk), :], cache_ref[pl.ds(start, tk), :])

    carry = (jnp.zeros((1, ncol), F32), jnp.zeros((ncol, KV_LORA), F32))
    carry = lax.fori_loop(0, n_tiles, pass2, carry)
    l, ctx = accumulate(carry, s_new, c_new)

    eye = lax.broadcasted_iota(jnp.int32, (ncol, ncol), 0) == lax.broadcasted_iota(jnp.int32, (ncol, ncol), 1)
    l_col = jnp.sum(jnp.where(eye, jnp.broadcast_to(l, (ncol, ncol)), 0.0), axis=1, keepdims=True)
    ctx = (ctx / l_col).astype(BF16)
    r = jnp.dot(ctx, wuv_ref[...], preferred_element_type=F32)
    col_head = lax.broadcasted_iota(jnp.int32, (t_new, ATTN_WIDTH), 1) // V_HEAD
    out = jnp.zeros((t_new, ATTN_WIDTH), F32)
    for hd in range(N_HEADS):
        out = out + jnp.where(col_head == hd, r[hd * t_new:(hd + 1) * t_new, :], 0.0)
    o_ref[...] = out.astype(BF16)


def _cache_attn(q, cache_ckv, cache_krope_t, layer, c_new, kr_new, w, nseq, t_new, tk):
    past = cache_ckv.shape[2]
    assert past % tk == 0
    ncol = N_HEADS * t_new
    return pl.pallas_call(
        functools.partial(_cache_attn_kernel, past=past, t_new=t_new, tk=tk),
        grid=(nseq,),
        in_specs=[
            pl.BlockSpec((t_new, SLAB), lambda b: (b, 0)),
            pl.BlockSpec((None, None, past, KV_LORA), lambda b: (layer, b, 0, 0)),
            pl.BlockSpec((None, None, QK_ROPE, past), lambda b: (layer, b, 0, 0)),
            pl.BlockSpec((t_new, KV_LORA), lambda b: (b, 0)),
            pl.BlockSpec((t_new, QK_ROPE), lambda b: (b, 0)),
        ] + [_const_spec(a.shape) for a in (w["gk1"], w["wc"], w["wuk"], w["e"], w["wv"])],
        out_specs=pl.BlockSpec((t_new, ATTN_WIDTH), lambda b: (b, 0)),
        out_shape=jax.ShapeDtypeStruct((nseq * t_new, ATTN_WIDTH), BF16),
        scratch_shapes=[pltpu.VMEM((past, ncol), F32)],
        compiler_params=pltpu.CompilerParams(dimension_semantics=("parallel",),
                                             vmem_limit_bytes=VMEM_LIMIT),
        name="cache_attn",
    )(q, cache_ckv, cache_krope_t, c_new, kr_new, w["gk1"], w["wc"], w["wuk"], w["e"], w["wv"])


def _merge_kernel(*refs, groups, rows, tiles_per_seq, pos0, has_hist):
    if has_hist:
        x_ref, a_ref, hist_ref = refs[:3]
        refs = refs[3:]
    else:
        x_ref, a_ref = refs[:2]
        hist_ref = None
        refs = refs[2:]
    (gmix_ref, w2_ref, wpool_ref, pscale_ref, wao_ref, wpo_ref, wo_ref, gmlp_ref, wup_ref, wdown_ref,
     y_ref, pool_ref, ext_ref) = refs
    tm = groups * rows
    tile = pl.program_id(0) % tiles_per_seq

    x = x_ref[...]
    h = _rms(x, gmix_ref[...]).astype(BF16)
    z = jnp.dot(h, w2_ref[...], preferred_element_type=F32)
    d = x.shape[1]
    p = z[:, :POOL_WIDTH]
    p3 = p.reshape(groups, rows, POOL_WIDTH)

    if has_hist:
        ext_ref[:, :POOL_PAD, :] = hist_ref[...]
    else:
        @pl.when(tile == 0)
        def _():
            ext_ref[:, :POOL_PAD, :] = jnp.zeros((groups, POOL_PAD, POOL_WIDTH), F32)

        @pl.when(tile != 0)
        def _():
            ext_ref[:, :POOL_PAD, :] = ext_ref[:, rows:rows + POOL_PAD, :]
    ext_ref[:, POOL_PAD:, :] = p3
    pool_ref[...] = p3[:, rows - POOL_PAD:, :]

    pos = pos0 + tile * rows + lax.broadcasted_iota(jnp.int32, (groups, rows, POOL_GROUP), 1)
    us = []
    for g, win in enumerate(POOL_WINDOWS):
        sl = slice(g * POOL_GROUP, (g + 1) * POOL_GROUP)
        tot = p3[:, :, sl]
        for j in range(1, win):
            tot = tot + ext_ref[:, pl.ds(POOL_PAD - j, rows), sl]
        cnt = jnp.minimum(win, pos + 1).astype(F32)
        pooled = (tot / cnt - p3[:, :, sl]).reshape(tm, POOL_GROUP)
        u = jnp.dot(pooled.astype(BF16), wpool_ref[g], preferred_element_type=F32)
        us.append((u * pscale_ref[:, sl]).astype(BF16))
    u = jnp.concatenate(us, axis=1)

    branch_a = jnp.dot(a_ref[...], wao_ref[...], preferred_element_type=F32)
    branch_b = jnp.dot(u, wpo_ref[...], preferred_element_type=F32)
    gate_a = 1.0 / (1.0 + jnp.exp(-z[:, POOL_WIDTH:POOL_WIDTH + d]))
    gate_b = 1.0 / (1.0 + jnp.exp(-z[:, POOL_WIDTH + d:]))
    mix = (gate_a * branch_a + gate_b * branch_b).astype(BF16)
    x = x + jnp.dot(mix, wo_ref[...], preferred_element_type=F32)
    hm = _rms(x, gmlp_ref[...]).astype(BF16)
    up = jnp.dot(hm, wup_ref[...], preferred_element_type=F32)
    act = jnp.square(jnp.maximum(up, 0.0)).astype(BF16)
    y_ref[...] = x + jnp.dot(act, wdown_ref[...], preferred_element_type=F32)


def _merge(x, attn, hist, w, nseq, seq, tm, pos0):
    t, d = x.shape
    assert t % tm == 0
    if hist is None:
        assert seq % tm == 0
        groups, rows, tiles_per_seq = 1, tm, seq // tm
    else:
        assert tm % seq == 0 and seq >= POOL_PAD
        groups, rows, tiles_per_seq = tm // seq, seq, 1
    assert rows % 8 == 0 and rows >= POOL_PAD
    row = lambda i: (i, 0)
    seq_block = lambda i: (i // tiles_per_seq, 0, 0)
    in_specs = [pl.BlockSpec((tm, d), row), pl.BlockSpec((tm, ATTN_WIDTH), row)]
    args = [x, attn]
    if hist is not None:
        in_specs.append(pl.BlockSpec((groups, POOL_PAD, POOL_WIDTH), seq_block))
        args.append(hist)
    weights = (w["g_mix"], w["w2"], w["w_pool"], w["pool_scale"], w["w_attn_out"], w["w_pool_out"],
               w["w_o"], w["g_mlp"], w["w_up"], w["w_down"])
    in_specs += [_const_spec(a.shape) for a in weights]
    return pl.pallas_call(
        functools.partial(_merge_kernel, groups=groups, rows=rows, tiles_per_seq=tiles_per_seq,
                          pos0=pos0, has_hist=hist is not None),
        grid=(t // tm,),
        in_specs=in_specs,
        out_specs=(pl.BlockSpec((tm, d), row),
                   pl.BlockSpec((groups, POOL_PAD, POOL_WIDTH), seq_block)),
        out_shape=(jax.ShapeDtypeStruct((t, d), F32),
                   jax.ShapeDtypeStruct((nseq, POOL_PAD, POOL_WIDTH), F32)),
        scratch_shapes=[pltpu.VMEM((groups, POOL_PAD + rows, POOL_WIDTH), F32)],
        compiler_params=pltpu.CompilerParams(dimension_semantics=("arbitrary",),
                                             vmem_limit_bytes=VMEM_LIMIT),
        name="merge",
    )(*args, *weights)


def _rope_tables(pos):
    half = QK_ROPE // 2
    inv = jnp.power(ROPE_THETA, -jnp.arange(half, dtype=F32) / half)
    ang = pos[:, None] * inv[None, :]
    cos, sin = jnp.cos(ang), jnp.sin(ang)
    n = pos.shape[0]
    one = jnp.ones((n, LANE - QK_ROPE), F32)
    zero = jnp.zeros((n, LANE - half), F32)
    c = jnp.concatenate([cos, cos, one], axis=1)
    s1 = jnp.concatenate([jnp.zeros((n, half), F32), sin, jnp.zeros((n, LANE - QK_ROPE), F32)], axis=1)
    s2 = jnp.concatenate([-sin, zero], axis=1)
    return c, s1, s2, cos.T, sin.T


def _slab(rope_part, nope_part):
    pad = jnp.zeros(rope_part.shape[:-1] + (LANE - QK_HEAD,), rope_part.dtype)
    s = jnp.concatenate([rope_part, nope_part, pad], axis=-1)
    return s.reshape(s.shape[:-2] + (N_HEADS * LANE,))


def _layer_weights(l, t_new, g_mix, w_in, g_qa, g_kva, w_uq, w_ukv, g_q, g_k, w_attn_out, w_pool,
                   pool_scale, w_pool_out, w_o, g_mlp, w_up, w_down):
    off_kv, off_kr = Q_LORA, Q_LORA + KV_LORA
    off_p = off_kr + QK_ROPE
    wi = w_in[l]
    d = wi.shape[0]
    w1 = jnp.concatenate([wi[:, :off_p], jnp.zeros((d, LANE - QK_ROPE), F32)], axis=1)
    uq = w_uq[l]
    ukv = w_ukv[l]
    k_nope, v = ukv[..., :QK_NOPE], ukv[..., QK_NOPE:]
    zr = jnp.zeros(k_nope.shape[:-1] + (QK_ROPE,), F32)
    gq, gk = g_q[l], g_k[l]
    slab_gain = lambda g: jnp.concatenate([g[QK_NOPE:], g[:QK_NOPE], jnp.zeros((LANE - QK_HEAD,), F32)])[None]
    wc = jnp.zeros((N_HEADS, LANE, KV_LORA), F32).at[:, QK_ROPE:QK_HEAD, :].set(k_nope.transpose(1, 2, 0))
    ncol = N_HEADS * t_new
    e = (jnp.arange(N_HEADS * QK_NOPE)[:, None] // QK_NOPE == jnp.arange(ncol)[None, :] // t_new)
    return {
        "g_mix": g_mix[l][None],
        "w1": w1.astype(BF16),
        "w2": wi[:, off_p:].astype(BF16),
        "g_qa": g_qa[l][None],
        "g_kva": g_kva[l][None],
        "wq": _slab(uq[..., QK_NOPE:], uq[..., :QK_NOPE]).astype(BF16),
        "wq_t": _slab(uq[..., QK_NOPE:], uq[..., :QK_NOPE]).T.astype(BF16),
        "wk": _slab(zr, k_nope).astype(BF16),
        "wv": v.reshape(KV_LORA, ATTN_WIDTH).astype(BF16),
        "wv_t": v.reshape(KV_LORA, ATTN_WIDTH).T.astype(BF16),
        "wuk": k_nope.reshape(KV_LORA, N_HEADS * QK_NOPE).astype(BF16),
        "wc": wc.astype(BF16),
        "e": e.astype(BF16),
        "gq": slab_gain(gq) * (SM_SCALE * LOG2E),
        "gq_t": slab_gain(gq).T * (SM_SCALE * LOG2E),
        "gk": slab_gain(gk),
        "gk1": slab_gain(gk),
        "w_attn_out": w_attn_out[l].astype(BF16),
        "w_pool": w_pool[l].astype(BF16),
        "pool_scale": pool_scale[l][None],
        "w_pool_out": w_pool_out[l].astype(BF16),
        "w_o": w_o[l].astype(BF16),
        "g_mlp": g_mlp[l][None],
        "w_up": w_up[l].astype(BF16),
        "w_down": w_down[l].astype(BF16),
    }


def kernel(x_prompt, x_sample, cache_ckv, cache_krope, state_pool, g_mix, w_in, g_qa, g_kva, w_uq, w_ukv,
           g_q, g_k, w_attn_out, w_pool, pool_scale, w_pool_out, w_o, g_mlp, w_up, w_down):
    batch, seq, d = x_prompt.shape
    nseq, t_new, _ = x_sample.shape
    depth = g_mix.shape[0]
    past = cache_ckv.shape[2]

    tabs_p = _rope_tables(jnp.arange(seq, dtype=F32))
    proj_tm_s = min(PROJ_TM, nseq * t_new)
    reps = proj_tm_s // t_new
    tabs_s = _rope_tables(past + jnp.arange(t_new, dtype=F32))
    tabs_s = tuple(jnp.tile(a, (reps, 1)) for a in tabs_s[:3]) + tuple(jnp.tile(a, (1, reps)) for a in tabs_s[3:])
    hist = jnp.pad(state_pool, ((0, 0), (0, 0), (POOL_PAD - POOL_HIST, 0), (0, 0)))
    krope_t = jnp.swapaxes(cache_krope, 2, 3)

    xp = x_prompt.reshape(batch * seq, d)
    xs = x_sample.reshape(nseq * t_new, d)
    outs = {k: [] for k in ("kr_p", "pool_p", "kr_s", "pool_s")}
    ckv_p = ckv_s = None
    for l in range(depth):
        w = _layer_weights(l, t_new, g_mix, w_in, g_qa, g_kva, w_uq, w_ukv, g_q, g_k, w_attn_out, w_pool,
                           pool_scale, w_pool_out, w_o, g_mlp, w_up, w_down)
        qt, k, vt, ckv_p, kr = _proj(xp, tabs_p, seq // PROJ_TM, w, PROJ_TM, True, l, depth, ckv_p)
        attn = _flash(qt, k, vt, batch, seq)
        xp, pool = _merge(xp, attn, None, w, batch, seq, MERGE_TM, 0)
        outs["kr_p"].append(kr.reshape(batch, seq, QK_ROPE))
        outs["pool_p"].append(pool[:, POOL_PAD - POOL_HIST:])

        q, ckv_s, kr = _proj(xs, tabs_s, 1, w, proj_tm_s, False, l, depth, ckv_s)
        attn = _cache_attn(q, cache_ckv, krope_t, l, ckv_s[l], kr, w, nseq, t_new, CACHE_TK)
        xs, pool = _merge(xs, attn, hist[l], w, nseq, t_new, min(MERGE_TM, nseq * t_new), past)
        outs["kr_s"].append(kr.reshape(nseq, t_new, QK_ROPE))
        outs["pool_s"].append(pool[:, POOL_PAD - POOL_HIST:])

    return (xp.reshape(batch, seq, d), xs.reshape(nseq, t_new, d),
            ckv_p.reshape(depth, batch, seq, KV_LORA), jnp.stack(outs["kr_p"]), jnp.stack(outs["pool_p"]),
            ckv_s.reshape(depth, nseq, t_new, KV_LORA), jnp.stack(outs["kr_s"]), jnp.stack(outs["pool_s"]))
```

```python
import functools

import jax
import jax.numpy as jnp
from jax import lax
from jax.experimental import pallas as pl
from jax.experimental.pallas import tpu as pltpu

F32 = jnp.float32
BF16 = jnp.bfloat16

LANE = 128
CHUNK = 64
N_HEADS = 8
QK_NOPE = 64
QK_ROPE = 32
QK_HEAD = QK_NOPE + QK_ROPE
V_HEAD = 64
Q_LORA = 384
KV_LORA = 256
ATTN_WIDTH = N_HEADS * V_HEAD
VT_ROWS = V_HEAD + 16
POOL_WINDOWS = (2, 4, 8, 16)
POOL_GROUP = 128
POOL_WIDTH = len(POOL_WINDOWS) * POOL_GROUP
POOL_HIST = max(POOL_WINDOWS) - 1
POOL_PAD = POOL_HIST + 1
ROPE_THETA = 10000.0
EPS = 1e-6
SM_SCALE = QK_HEAD ** -0.5
LOG2E = 1.4426950408889634
SLAB = N_HEADS * LANE
NEG = -1e30

PROJ_TM = 512
ATT_T = 256
S_PAD_ROWS = 8
MERGE_TM = 512
CACHE_TK = 512
VMEM_LIMIT = 56 * 1024 * 1024


def _const_spec(shape):
    nd = len(shape)
    return pl.BlockSpec(shape, lambda *_: (0,) * nd, pipeline_mode=pl.Buffered(1))


def _rms(x, g):
    ms = jnp.mean(x * x, axis=-1, keepdims=True)
    return x * lax.rsqrt(ms + EPS) * g


def _rope_slab(x, c, s1, s2):
    return x * c + pltpu.roll(x, 16, 1) * s1 + pltpu.roll(x, LANE - 16, 1) * s2


def _head_norm(x, g):
    ss = jnp.sum(x * x, axis=-1, keepdims=True)
    return x * lax.rsqrt(ss * (1.0 / QK_HEAD) + EPS) * g


def _proj_kernel(x_ref, c_ref, s1_ref, s2_ref, ct_ref, st_ref, gmix_ref, w1_ref, gqa_ref, gkva_ref, wq_ref,
                 wqt_ref, wk_ref, wvt_ref, gq_ref, gqt_ref, gk_ref, *out_refs, transposed, aliased):
    if aliased:
        out_refs = out_refs[1:]
    if transposed:
        q_ref, k_ref, v_ref, ckv_ref, kr_ref = out_refs
    else:
        q_ref, ckv_ref, kr_ref = out_refs
    half = QK_ROPE // 2
    for t in range(x_ref.shape[0] // ATT_T):
        rows = slice(t * ATT_T, (t + 1) * ATT_T)
        h = _rms(x_ref[rows, :], gmix_ref[...]).astype(BF16)
        z = jnp.dot(h, w1_ref[...], preferred_element_type=F32)
        cq = _rms(z[:, :Q_LORA], gqa_ref[...]).astype(BF16)
        ckv = _rms(z[:, Q_LORA:Q_LORA + KV_LORA], gkva_ref[...])
        ckv_ref[rows, :] = ckv
        c, s1, s2 = c_ref[rows, :], s1_ref[rows, :], s2_ref[rows, :]
        kr = _rope_slab(z[:, Q_LORA + KV_LORA:], c, s1, s2)
        kr_ref[rows, :] = kr[:, :QK_ROPE]
        if not transposed:
            q = jnp.dot(cq, wq_ref[...], preferred_element_type=F32)
            for hd in range(N_HEADS):
                sl = slice(hd * LANE, (hd + 1) * LANE)
                q_ref[rows, sl] = _head_norm(_rope_slab(q[:, sl], c, s1, s2), gq_ref[...]).astype(BF16)
            continue
        ckv_b = ckv.astype(BF16)
        kn = jnp.dot(ckv_b, wk_ref[...], preferred_element_type=F32)
        nt = (((1,), (1,)), ((), ()))
        vt = lax.dot_general(wvt_ref[...], ckv_b, nt, preferred_element_type=F32).astype(BF16)
        qt_all = lax.dot_general(wqt_ref[...], cq, nt, preferred_element_type=F32)
        ones = jnp.ones((VT_ROWS - V_HEAD, ATT_T), BF16)
        cos, sin = ct_ref[:, rows], st_ref[:, rows]
        for hd in range(N_HEADS):
            sl = slice(hd * LANE, (hd + 1) * LANE)
            v_ref[t, hd * VT_ROWS:hd * VT_ROWS + V_HEAD, :] = vt[hd * V_HEAD:(hd + 1) * V_HEAD, :]
            v_ref[t, hd * VT_ROWS + V_HEAD:(hd + 1) * VT_ROWS, :] = ones
            qt = qt_all[sl, :]
            x1, x2 = qt[:half], qt[half:QK_ROPE]
            qt = jnp.concatenate([x1 * cos - x2 * sin, x1 * sin + x2 * cos, qt[QK_ROPE:]], axis=0)
            ss = jnp.sum(qt * qt, axis=0, keepdims=True)
            q_ref[t, sl, :] = (qt * lax.rsqrt(ss * (1.0 / QK_HEAD) + EPS) * gqt_ref[...]).astype(BF16)
            k_ref[rows, sl] = _head_norm(kn[:, sl] + kr, gk_ref[...]).astype(BF16)


def _proj(x, tabs, tab_tiles, w, tm, transposed, layer, depth, ckv_all):
    t = x.shape[0]
    d = x.shape[1]
    assert t % tm == 0 and tm % ATT_T == 0
    c_tab, s1_tab, s2_tab, cos_t, sin_t = tabs
    row = lambda i: (i, 0)
    lead = lambda i: (i, 0, 0)
    tab = pl.BlockSpec((tm, LANE), lambda i: (i % tab_tiles, 0))
    tab_t = pl.BlockSpec((QK_ROPE // 2, tm), lambda i: (0, i % tab_tiles))
    weights = (w["g_mix"], w["w1"], w["g_qa"], w["g_kva"], w["wq"], w["wq_t"], w["wk"], w["wv_t"], w["gq"],
               w["gq_t"], w["gk"])
    in_specs = [pl.BlockSpec((tm, d), row), tab, tab, tab, tab_t, tab_t] + [
        _const_spec(a.shape) for a in weights]
    latent_shapes = (jax.ShapeDtypeStruct((depth, t, KV_LORA), F32), jax.ShapeDtypeStruct((t, QK_ROPE), F32))
    latent_specs = (pl.BlockSpec((None, tm, KV_LORA), lambda i: (layer, i, 0)), pl.BlockSpec((tm, QK_ROPE), row))
    if transposed:
        nt = tm // ATT_T
        out_shape = (
            jax.ShapeDtypeStruct((t // ATT_T, SLAB, ATT_T), BF16),
            jax.ShapeDtypeStruct((t, SLAB), BF16),
            jax.ShapeDtypeStruct((t // ATT_T, N_HEADS * VT_ROWS, ATT_T), BF16),
        ) + latent_shapes
        out_specs = (
            pl.BlockSpec((nt, SLAB, ATT_T), lead),
            pl.BlockSpec((tm, SLAB), row),
            pl.BlockSpec((nt, N_HEADS * VT_ROWS, ATT_T), lead),
        ) + latent_specs
    else:
        out_shape = (jax.ShapeDtypeStruct((t, SLAB), BF16),) + latent_shapes
        out_specs = (pl.BlockSpec((tm, SLAB), row),) + latent_specs
    args = [x, c_tab, s1_tab, s2_tab, cos_t, sin_t, *weights]
    aliases = {}
    if ckv_all is not None:
        in_specs.append(pl.BlockSpec(memory_space=pl.ANY))
        aliases = {len(args): len(out_shape) - 2}
        args.append(ckv_all)
    return pl.pallas_call(
        functools.partial(_proj_kernel, transposed=transposed, aliased=ckv_all is not None),
        grid=(t // tm,),
        in_specs=in_specs,
        out_specs=out_specs,
        out_shape=out_shape,
        input_output_aliases=aliases,
        compiler_params=pltpu.CompilerParams(dimension_semantics=("parallel",),
                                             vmem_limit_bytes=VMEM_LIMIT),
        name="proj",
    )(*args)


def _flash_kernel(qt_ref, k_ref, vt_ref, o_ref, m_ref, acc_ref, s_ref, mt_ref):
    t = ATT_T
    i = pl.program_id(1)
    m_ref[...] = jnp.full(m_ref.shape, NEG, F32)
    acc_ref[...] = jnp.zeros(acc_ref.shape, F32)

    def score_head(slot, tile, masked, hd):
        start = pl.multiple_of(tile * t, t)
        qsl = slice(hd * LANE, (hd + 1) * LANE)
        s = jnp.dot(k_ref[pl.ds(start, t), qsl], qt_ref[qsl, :], preferred_element_type=F32)
        if masked:
            k_chunk = lax.broadcasted_iota(jnp.int32, (t, t), 0) // CHUNK
            q_chunk = lax.broadcasted_iota(jnp.int32, (t, t), 1) // CHUNK
            s = jnp.where(k_chunk <= q_chunk, s, NEG)
        s_ref[slot, hd, :t, :] = s
        mt_ref[slot, hd:hd + 1, :] = jnp.max(s, axis=0, keepdims=True)

    def value_head(slot, tile, hd):
        vsl = slice(hd * VT_ROWS, (hd + 1) * VT_ROWS)
        m_old = m_ref[hd:hd + 1, :]
        m_new = jnp.maximum(m_old, mt_ref[slot, hd:hd + 1, :])
        alpha = jnp.exp2(m_old - m_new)
        p = jnp.exp2(s_ref[slot, hd, :t, :] - m_new).astype(BF16)
        pv = jnp.dot(vt_ref[tile, vsl, :], p, preferred_element_type=F32)
        acc_ref[vsl, :] = alpha * acc_ref[vsl, :] + pv
        m_ref[hd:hd + 1, :] = m_new

    def score_phase(slot, tile, masked):
        for hd in range(N_HEADS):
            score_head(slot, tile, masked, hd)

    def value_phase(slot, tile):
        for hd in range(N_HEADS):
            value_head(slot, tile, hd)

    def both_phases(score_slot, score_tile, value_slot, value_tile):
        for hd in range(N_HEADS):
            score_head(score_slot, score_tile, False, hd)
            value_head(value_slot, value_tile, hd)

    def tile_pair(first, pending):
        both_phases(1, first, 0, pending)
        both_phases(0, first + 1, 1, first)

    score_phase(0, i, True)
    n_quads = i // 4

    def quad(p, carry):
        tile_pair(4 * p, jnp.where(p == 0, i, 4 * p - 1))
        tile_pair(4 * p + 2, 4 * p + 1)
        return carry

    lax.fori_loop(0, n_quads, quad, 0)
    done = 4 * n_quads
    pending = jnp.where(n_quads == 0, i, done - 1)
    has_pair = i - done >= 2

    @pl.when(has_pair)
    def _():
        tile_pair(done, pending)

    done = jnp.where(has_pair, done + 2, done)
    pending = jnp.where(has_pair, done - 1, pending)

    @pl.when(i > done)
    def _():
        both_phases(1, done, 0, pending)
        value_phase(1, done)

    @pl.when(i == done)
    def _():
        value_phase(0, pending)
    outs = []
    for hd in range(N_HEADS):
        base = hd * VT_ROWS
        outs.append(acc_ref[base:base + V_HEAD, :] / acc_ref[base + V_HEAD:base + V_HEAD + 1, :])
    o_ref[...] = jnp.concatenate(outs, axis=0).T.astype(BF16)


def _flash(qt, k, vt, batch, seq):
    assert seq % ATT_T == 0 and ATT_T % CHUNK == 0
    nq = seq // ATT_T
    k3 = k.reshape(batch, seq, SLAB)
    return pl.pallas_call(
        _flash_kernel,
        grid=(batch, nq),
        in_specs=[
            pl.BlockSpec((None, SLAB, ATT_T), lambda b, i: (b * nq + i, 0, 0)),
            pl.BlockSpec((None, seq, SLAB), lambda b, i: (b, 0, 0)),
            pl.BlockSpec((nq, N_HEADS * VT_ROWS, ATT_T), lambda b, i: (b, 0, 0)),
        ],
        out_specs=pl.BlockSpec((ATT_T, ATTN_WIDTH), lambda b, i: (b * nq + i, 0)),
        out_shape=jax.ShapeDtypeStruct((batch * seq, ATTN_WIDTH), BF16),
        scratch_shapes=[pltpu.VMEM((N_HEADS, ATT_T), F32), pltpu.VMEM((N_HEADS * VT_ROWS, ATT_T), F32),
                        pltpu.VMEM((2, N_HEADS, ATT_T + S_PAD_ROWS, ATT_T), F32),
                        pltpu.VMEM((2, N_HEADS, ATT_T), F32)],
        compiler_params=pltpu.CompilerParams(dimension_semantics=("parallel", "parallel"),
                                             vmem_limit_bytes=VMEM_LIMIT),
        name="flash",
    )(qt, k3, vt)


def _cache_attn_kernel(q_ref, cache_ref, ckr_ref, cnew_ref, krnew_ref, gk_ref, wc_ref, wuk_ref, e_ref,
                       wuv_ref, o_ref, s_ref, *, past, t_new, tk):
    n_tiles = past // tk
    ncol = N_HEADS * t_new
    gk = gk_ref[...]
    qt, qr = [], []
    for hd in range(N_HEADS):
        qg = q_ref[:, hd * LANE:(hd + 1) * LANE].astype(F32) * gk
        qt.append(jnp.dot(qg.astype(BF16), wc_ref[hd], preferred_element_type=F32))
        qr.append(qg[:, :QK_ROPE])
    qt = jnp.concatenate(qt, axis=0).astype(BF16)
    qr = jnp.concatenate(qr, axis=0).astype(BF16)

    nt = (((1,), (1,)), ((), ()))

    def scores_many(tiles):
        cbs = [c.astype(BF16) for c, _ in tiles]
        kns = [jnp.dot(cb, wuk_ref[...], preferred_element_type=F32) for cb in cbs]
        ssqs = [jnp.dot((kn * kn).astype(BF16), e_ref[...], preferred_element_type=F32) for kn in kns]
        invs = [lax.rsqrt((ssq + jnp.sum(kr * kr, axis=-1, keepdims=True)) * (1.0 / QK_HEAD) + EPS)
                for ssq, (_, kr) in zip(ssqs, tiles)]
        ss = [lax.dot_general(cb, qt, nt, preferred_element_type=F32) for cb in cbs]
        ss = [s + lax.dot_general(kr.astype(BF16), qr, nt, preferred_element_type=F32)
              for s, (_, kr) in zip(ss, tiles)]
        return [s * inv for s, inv in zip(ss, invs)]

    def scores(c, kr):
        return scores_many([(c, kr)])[0]

    unroll = 2 if n_tiles % 2 == 0 else 1

    def pass1(j, m):
        starts = [pl.multiple_of((j * unroll + u) * tk, tk) for u in range(unroll)]
        tiles = [(cache_ref[pl.ds(st, tk), :], ckr_ref[:, pl.ds(st, tk)].T) for st in starts]
        for st, s in zip(starts, scores_many(tiles)):
            s_ref[pl.ds(st, tk), :] = s
            m = jnp.maximum(m, jnp.max(s, axis=0, keepdims=True))
        return m

    m = lax.fori_loop(0, n_tiles // unroll, pass1, jnp.full((1, ncol), NEG, F32))
    c_new = cnew_ref[...]
    s_new = scores(c_new, krnew_ref[...])
    m = jnp.maximum(m, jnp.max(s_new, axis=0, keepdims=True))

    tn = (((0,), (0,)), ((), ()))

    def accumulate(s, c):
        p = jnp.exp2(s - m)
        ctx = lax.dot_general(p.astype(BF16), c.astype(BF16), tn, preferred_element_type=F32)
        return jnp.sum(p, axis=0, keepdims=True), ctx

    l_old, ctx_old = accumulate(s_ref[...], cache_ref[...])
    l_new, ctx_new = accumulate(s_new, c_new)
    l, ctx = l_old + l_new, ctx_old + ctx_new

    eye = lax.broadcasted_iota(jnp.int32, (ncol, ncol), 0) == lax.broadcasted_iota(jnp.int32, (ncol, ncol), 1)
    l_col = jnp.sum(jnp.where(eye, jnp.broadcast_to(l, (ncol, ncol)), 0.0), axis=1, keepdims=True)
    ctx = (ctx / l_col).astype(BF16)
    r = jnp.dot(ctx, wuv_ref[...], preferred_element_type=F32)
    col_head = lax.broadcasted_iota(jnp.int32, (t_new, ATTN_WIDTH), 1) // V_HEAD
    out = jnp.zeros((t_new, ATTN_WIDTH), F32)
    for hd in range(N_HEADS):
        out = out + jnp.where(col_head == hd, r[hd * t_new:(hd + 1) * t_new, :], 0.0)
    o_ref[...] = out.astype(BF16)


def _cache_attn(q, cache_ckv, cache_krope_t, layer, c_new, kr_new, w, nseq, t_new, tk):
    past = cache_ckv.shape[2]
    assert past % tk == 0
    ncol = N_HEADS * t_new
    return pl.pallas_call(
        functools.partial(_cache_attn_kernel, past=past, t_new=t_new, tk=tk),
        grid=(nseq,),
        in_specs=[
            pl.BlockSpec((t_new, SLAB), lambda b: (b, 0)),
            pl.BlockSpec((None, None, past, KV_LORA), lambda b: (layer, b, 0, 0)),
            pl.BlockSpec((None, None, QK_ROPE, past), lambda b: (layer, b, 0, 0)),
            pl.BlockSpec((t_new, KV_LORA), lambda b: (b, 0)),
            pl.BlockSpec((t_new, QK_ROPE), lambda b: (b, 0)),
        ] + [_const_spec(a.shape) for a in (w["gk1"], w["wc"], w["wuk"], w["e"], w["wv"])],
        out_specs=pl.BlockSpec((t_new, ATTN_WIDTH), lambda b: (b, 0)),
        out_shape=jax.ShapeDtypeStruct((nseq * t_new, ATTN_WIDTH), BF16),
        scratch_shapes=[pltpu.VMEM((past, ncol), F32)],
        compiler_params=pltpu.CompilerParams(dimension_semantics=("parallel",),
                                             vmem_limit_bytes=VMEM_LIMIT),
        name="cache_attn",
    )(q, cache_ckv, cache_krope_t, c_new, kr_new, w["gk1"], w["wc"], w["wuk"], w["e"], w["wv"])


def _merge_kernel(*refs, groups, rows, tiles_per_seq, pos0, has_hist):
    if has_hist:
        x_ref, a_ref, hist_ref = refs[:3]
        refs = refs[3:]
    else:
        x_ref, a_ref = refs[:2]
        hist_ref = None
        refs = refs[2:]
    (gmix_ref, w2_ref, wpool_ref, pscale_ref, wao_ref, wpo_ref, wo_ref, gmlp_ref, wup_ref, wdown_ref,
     y_ref, pool_ref, ext_ref) = refs
    tm = groups * rows
    tile = pl.program_id(0) % tiles_per_seq

    x = x_ref[...]
    h = _rms(x, gmix_ref[...]).astype(BF16)
    z = jnp.dot(h, w2_ref[...], preferred_element_type=F32)
    d = x.shape[1]
    p = z[:, :POOL_WIDTH]
    p3 = p.reshape(groups, rows, POOL_WIDTH)

    if has_hist:
        ext_ref[:, :POOL_PAD, :] = hist_ref[...]
    else:
        @pl.when(tile == 0)
        def _():
            ext_ref[:, :POOL_PAD, :] = jnp.zeros((groups, POOL_PAD, POOL_WIDTH), F32)

        @pl.when(tile != 0)
        def _():
            ext_ref[:, :POOL_PAD, :] = ext_ref[:, rows:rows + POOL_PAD, :]
    ext_ref[:, POOL_PAD:, :] = p3
    pool_ref[...] = p3[:, rows - POOL_PAD:, :]

    pos = pos0 + tile * rows + lax.broadcasted_iota(jnp.int32, (groups, rows, POOL_GROUP), 1)
    us = []
    for g, win in enumerate(POOL_WINDOWS):
        sl = slice(g * POOL_GROUP, (g + 1) * POOL_GROUP)
        tot = p3[:, :, sl]
        for j in range(1, win):
            tot = tot + ext_ref[:, pl.ds(POOL_PAD - j, rows), sl]
        cnt = jnp.minimum(win, pos + 1).astype(F32)
        pooled = (tot / cnt - p3[:, :, sl]).reshape(tm, POOL_GROUP)
        u = jnp.dot(pooled.astype(BF16), wpool_ref[g], preferred_element_type=F32)
        us.append((u * pscale_ref[:, sl]).astype(BF16))
    u = jnp.concatenate(us, axis=1)

    branch_a = jnp.dot(a_ref[...], wao_ref[...], preferred_element_type=F32)
    branch_b = jnp.dot(u, wpo_ref[...], preferred_element_type=F32)
    gate_a = 1.0 / (1.0 + jnp.exp(-z[:, POOL_WIDTH:POOL_WIDTH + d]))
    gate_b = 1.0 / (1.0 + jnp.exp(-z[:, POOL_WIDTH + d:]))
    mix = (gate_a * branch_a + gate_b * branch_b).astype(BF16)
    x = x + jnp.dot(mix, wo_ref[...], preferred_element_type=F32)
    hm = _rms(x, gmlp_ref[...]).astype(BF16)
    up = jnp.dot(hm, wup_ref[...], preferred_element_type=F32)
    act = jnp.square(jnp.maximum(up, 0.0)).astype(BF16)
    y_ref[...] = x + jnp.dot(act, wdown_ref[...], preferred_element_type=F32)


def _merge(x, attn, hist, w, nseq, seq, tm, pos0):
    t, d = x.shape
    assert t % tm == 0
    if hist is None:
        assert seq % tm == 0
        groups, rows, tiles_per_seq = 1, tm, seq // tm
    else:
        assert tm % seq == 0 and seq >= POOL_PAD
        groups, rows, tiles_per_seq = tm // seq, seq, 1
    assert rows % 8 == 0 and rows >= POOL_PAD
    row = lambda i: (i, 0)
    seq_block = lambda i: (i // tiles_per_seq, 0, 0)
    in_specs = [pl.BlockSpec((tm, d), row), pl.BlockSpec((tm, ATTN_WIDTH), row)]
    args = [x, attn]
    if hist is not None:
        in_specs.append(pl.BlockSpec((groups, POOL_PAD, POOL_WIDTH), seq_block))
        args.append(hist)
    weights = (w["g_mix"], w["w2"], w["w_pool"], w["pool_scale"], w["w_attn_out"], w["w_pool_out"],
               w["w_o"], w["g_mlp"], w["w_up"], w["w_down"])
    in_specs += [_const_spec(a.shape) for a in weights]
    return pl.pallas_call(
        functools.partial(_merge_kernel, groups=groups, rows=rows, tiles_per_seq=tiles_per_seq,
                          pos0=pos0, has_hist=hist is not None),
        grid=(t // tm,),
        in_specs=in_specs,
        out_specs=(pl.BlockSpec((tm, d), row),
                   pl.BlockSpec((groups, POOL_PAD, POOL_WIDTH), seq_block)),
        out_shape=(jax.ShapeDtypeStruct((t, d), F32),
                   jax.ShapeDtypeStruct((nseq, POOL_PAD, POOL_WIDTH), F32)),
        scratch_shapes=[pltpu.VMEM((groups, POOL_PAD + rows, POOL_WIDTH), F32)],
        compiler_params=pltpu.CompilerParams(dimension_semantics=("arbitrary",),
                                             vmem_limit_bytes=VMEM_LIMIT),
        name="merge",
    )(*args, *weights)


def _rope_tables(pos):
    half = QK_ROPE // 2
    inv = jnp.power(ROPE_THETA, -jnp.arange(half, dtype=F32) / half)
    ang = pos[:, None] * inv[None, :]
    cos, sin = jnp.cos(ang), jnp.sin(ang)
    n = pos.shape[0]
    one = jnp.ones((n, LANE - QK_ROPE), F32)
    zero = jnp.zeros((n, LANE - half), F32)
    c = jnp.concatenate([cos, cos, one], axis=1)
    s1 = jnp.concatenate([jnp.zeros((n, half), F32), sin, jnp.zeros((n, LANE - QK_ROPE), F32)], axis=1)
    s2 = jnp.concatenate([-sin, zero], axis=1)
    return c, s1, s2, cos.T, sin.T


def _slab(rope_part, nope_part):
    pad = jnp.zeros(rope_part.shape[:-1] + (LANE - QK_HEAD,), rope_part.dtype)
    s = jnp.concatenate([rope_part, nope_part, pad], axis=-1)
    return s.reshape(s.shape[:-2] + (N_HEADS * LANE,))


def _layer_weights(l, t_new, g_mix, w_in, g_qa, g_kva, w_uq, w_ukv, g_q, g_k, w_attn_out, w_pool,
                   pool_scale, w_pool_out, w_o, g_mlp, w_up, w_down):
    off_kv, off_kr = Q_LORA, Q_LORA + KV_LORA
    off_p = off_kr + QK_ROPE
    wi = w_in[l]
    d = wi.shape[0]
    w1 = jnp.concatenate([wi[:, :off_p], jnp.zeros((d, LANE - QK_ROPE), F32)], axis=1)
    uq = w_uq[l]
    ukv = w_ukv[l]
    k_nope, v = ukv[..., :QK_NOPE], ukv[..., QK_NOPE:]
    zr = jnp.zeros(k_nope.shape[:-1] + (QK_ROPE,), F32)
    gq, gk = g_q[l], g_k[l]
    slab_gain = lambda g: jnp.concatenate([g[QK_NOPE:], g[:QK_NOPE], jnp.zeros((LANE - QK_HEAD,), F32)])[None]
    wc = jnp.zeros((N_HEADS, LANE, KV_LORA), F32).at[:, QK_ROPE:QK_HEAD, :].set(k_nope.transpose(1, 2, 0))
    ncol = N_HEADS * t_new
    e = (jnp.arange(N_HEADS * QK_NOPE)[:, None] // QK_NOPE == jnp.arange(ncol)[None, :] // t_new)
    return {
        "g_mix": g_mix[l][None],
        "w1": w1.astype(BF16),
        "w2": wi[:, off_p:].astype(BF16),
        "g_qa": g_qa[l][None],
        "g_kva": g_kva[l][None],
        "wq": _slab(uq[..., QK_NOPE:], uq[..., :QK_NOPE]).astype(BF16),
        "wq_t": _slab(uq[..., QK_NOPE:], uq[..., :QK_NOPE]).T.astype(BF16),
        "wk": _slab(zr, k_nope).astype(BF16),
        "wv": v.reshape(KV_LORA, ATTN_WIDTH).astype(BF16),
        "wv_t": v.reshape(KV_LORA, ATTN_WIDTH).T.astype(BF16),
        "wuk": k_nope.reshape(KV_LORA, N_HEADS * QK_NOPE).astype(BF16),
        "wc": wc.astype(BF16),
        "e": e.astype(BF16),
        "gq": slab_gain(gq) * (SM_SCALE * LOG2E),
        "gq_t": slab_gain(gq).T * (SM_SCALE * LOG2E),
        "gk": slab_gain(gk),
        "gk1": slab_gain(gk),
        "w_attn_out": w_attn_out[l].astype(BF16),
        "w_pool": w_pool[l].astype(BF16),
        "pool_scale": pool_scale[l][None],
        "w_pool_out": w_pool_out[l].astype(BF16),
        "w_o": w_o[l].astype(BF16),
        "g_mlp": g_mlp[l][None],
        "w_up": w_up[l].astype(BF16),
        "w_down": w_down[l].astype(BF16),
    }


def kernel(x_prompt, x_sample, cache_ckv, cache_krope, state_pool, g_mix, w_in, g_qa, g_kva, w_uq, w_ukv,
           g_q, g_k, w_attn_out, w_pool, pool_scale, w_pool_out, w_o, g_mlp, w_up, w_down):
    batch, seq, d = x_prompt.shape
    nseq, t_new, _ = x_sample.shape
    depth = g_mix.shape[0]
    past = cache_ckv.shape[2]

    tabs_p = _rope_tables(jnp.arange(seq, dtype=F32))
    proj_tm_s = min(PROJ_TM, nseq * t_new)
    reps = proj_tm_s // t_new
    tabs_s = _rope_tables(past + jnp.arange(t_new, dtype=F32))
    tabs_s = tuple(jnp.tile(a, (reps, 1)) for a in tabs_s[:3]) + tuple(jnp.tile(a, (1, reps)) for a in tabs_s[3:])
    hist = jnp.pad(state_pool, ((0, 0), (0, 0), (POOL_PAD - POOL_HIST, 0), (0, 0)))
    krope_t = jnp.swapaxes(cache_krope, 2, 3)

    xp = x_prompt.reshape(batch * seq, d)
    xs = x_sample.reshape(nseq * t_new, d)
    outs = {k: [] for k in ("kr_p", "pool_p", "kr_s", "pool_s")}
    ckv_p = ckv_s = None
    for l in range(depth):
        w = _layer_weights(l, t_new, g_mix, w_in, g_qa, g_kva, w_uq, w_ukv, g_q, g_k, w_attn_out, w_pool,
                           pool_scale, w_pool_out, w_o, g_mlp, w_up, w_down)
        qt, k, vt, ckv_p, kr = _proj(xp, tabs_p, seq // PROJ_TM, w, PROJ_TM, True, l, depth, ckv_p)
        attn = _flash(qt, k, vt, batch, seq)
        xp, pool = _merge(xp, attn, None, w, batch, seq, MERGE_TM, 0)
        outs["kr_p"].append(kr.reshape(batch, seq, QK_ROPE))
        outs["pool_p"].append(pool[:, POOL_PAD - POOL_HIST:])

        q, ckv_s, kr = _proj(xs, tabs_s, 1, w, proj_tm_s, False, l, depth, ckv_s)
        attn = _cache_attn(q, cache_ckv, krope_t, l, ckv_s[l], kr, w, nseq, t_new, CACHE_TK)
        xs, pool = _merge(xs, attn, hist[l], w, nseq, t_new, min(MERGE_TM, nseq * t_new), past)
        outs["kr_s"].append(kr.reshape(nseq, t_new, QK_ROPE))
        outs["pool_s"].append(pool[:, POOL_PAD - POOL_HIST:])

    return (xp.reshape(batch, seq, d), xs.reshape(nseq, t_new, d),
            ckv_p.reshape(depth, batch, seq, KV_LORA), jnp.stack(outs["kr_p"]), jnp.stack(outs["pool_p"]),
            ckv_s.reshape(depth, nseq, t_new, KV_LORA), jnp.stack(outs["kr_s"]), jnp.stack(outs["pool_s"]))
```

```python
import functools

import jax
import jax.numpy as jnp
from jax import lax
from jax.experimental import pallas as pl
from jax.experimental.pallas import tpu as pltpu

F32 = jnp.float32
BF16 = jnp.bfloat16

LANE = 128
CHUNK = 64
N_HEADS = 8
QK_NOPE = 64
QK_ROPE = 32
QK_HEAD = QK_NOPE + QK_ROPE
V_HEAD = 64
Q_LORA = 384
KV_LORA = 256
ATTN_WIDTH = N_HEADS * V_HEAD
VT_ROWS = V_HEAD + 16
POOL_WINDOWS = (2, 4, 8, 16)
POOL_GROUP = 128
POOL_WIDTH = len(POOL_WINDOWS) * POOL_GROUP
POOL_HIST = max(POOL_WINDOWS) - 1
POOL_PAD = POOL_HIST + 1
ROPE_THETA = 10000.0
EPS = 1e-6
SM_SCALE = QK_HEAD ** -0.5
LOG2E = 1.4426950408889634
SLAB = N_HEADS * LANE
NEG = -1e30

PROJ_TM = 1024
ATT_T = 256
MERGE_TM = 512
CACHE_TK = 512
VMEM_LIMIT = 56 * 1024 * 1024


def _const_spec(shape):
    nd = len(shape)
    return pl.BlockSpec(shape, lambda *_: (0,) * nd, pipeline_mode=pl.Buffered(1))


def _rms(x, g):
    ms = jnp.mean(x * x, axis=-1, keepdims=True)
    return x * lax.rsqrt(ms + EPS) * g


def _rope_slab(x, c, s1, s2):
    return x * c + pltpu.roll(x, 16, 1) * s1 + pltpu.roll(x, LANE - 16, 1) * s2


def _head_norm(x, g):
    ss = jnp.sum(x * x, axis=-1, keepdims=True)
    return x * lax.rsqrt(ss * (1.0 / QK_HEAD) + EPS) * g


def _proj_kernel(x_ref, c_ref, s1_ref, s2_ref, ct_ref, st_ref, gmix_ref, w1_ref, gqa_ref, gkva_ref, wq_ref,
                 wqt_ref, wk_ref, wvt_ref, gq_ref, gqt_ref, gk_ref, *out_refs, transposed, aliased):
    if aliased:
        out_refs = out_refs[1:]
    if transposed:
        q_ref, k_ref, v_ref, ckv_ref, kr_ref = out_refs
    else:
        q_ref, ckv_ref, kr_ref = out_refs
    half = QK_ROPE // 2
    for t in range(x_ref.shape[0] // ATT_T):
        rows = slice(t * ATT_T, (t + 1) * ATT_T)
        h = _rms(x_ref[rows, :], gmix_ref[...]).astype(BF16)
        z = jnp.dot(h, w1_ref[...], preferred_element_type=F32)
        cq = _rms(z[:, :Q_LORA], gqa_ref[...]).astype(BF16)
        ckv = _rms(z[:, Q_LORA:Q_LORA + KV_LORA], gkva_ref[...])
        ckv_ref[rows, :] = ckv
        c, s1, s2 = c_ref[rows, :], s1_ref[rows, :], s2_ref[rows, :]
        kr = _rope_slab(z[:, Q_LORA + KV_LORA:], c, s1, s2)
        kr_ref[rows, :] = kr[:, :QK_ROPE]
        if not transposed:
            q = jnp.dot(cq, wq_ref[...], preferred_element_type=F32)
            for hd in range(N_HEADS):
                sl = slice(hd * LANE, (hd + 1) * LANE)
                q_ref[rows, sl] = _head_norm(_rope_slab(q[:, sl], c, s1, s2), gq_ref[...]).astype(BF16)
            continue
        ckv_b = ckv.astype(BF16)
        kn = jnp.dot(ckv_b, wk_ref[...], preferred_element_type=F32)
        nt = (((1,), (1,)), ((), ()))
        vt = lax.dot_general(wvt_ref[...], ckv_b, nt, preferred_element_type=F32).astype(BF16)
        qt_all = lax.dot_general(wqt_ref[...], cq, nt, preferred_element_type=F32)
        ones = jnp.ones((VT_ROWS - V_HEAD, ATT_T), BF16)
        cos, sin = ct_ref[:, rows], st_ref[:, rows]
        for hd in range(N_HEADS):
            sl = slice(hd * LANE, (hd + 1) * LANE)
            v_ref[t, hd * VT_ROWS:hd * VT_ROWS + V_HEAD, :] = vt[hd * V_HEAD:(hd + 1) * V_HEAD, :]
            v_ref[t, hd * VT_ROWS + V_HEAD:(hd + 1) * VT_ROWS, :] = ones
            qt = qt_all[sl, :]
            x1, x2 = qt[:half], qt[half:QK_ROPE]
            qt = jnp.concatenate([x1 * cos - x2 * sin, x1 * sin + x2 * cos, qt[QK_ROPE:]], axis=0)
            ss = jnp.sum(qt * qt, axis=0, keepdims=True)
            q_ref[t, sl, :] = (qt * lax.rsqrt(ss * (1.0 / QK_HEAD) + EPS) * gqt_ref[...]).astype(BF16)
            k_ref[rows, sl] = _head_norm(kn[:, sl] + kr, gk_ref[...]).astype(BF16)


def _proj(x, tabs, tab_tiles, w, tm, transposed, layer, depth, ckv_all):
    t = x.shape[0]
    d = x.shape[1]
    assert t % tm == 0 and tm % ATT_T == 0
    c_tab, s1_tab, s2_tab, cos_t, sin_t = tabs
    row = lambda i: (i, 0)
    lead = lambda i: (i, 0, 0)
    tab = pl.BlockSpec((tm, LANE), lambda i: (i % tab_tiles, 0))
    tab_t = pl.BlockSpec((QK_ROPE // 2, tm), lambda i: (0, i % tab_tiles))
    weights = (w["g_mix"], w["w1"], w["g_qa"], w["g_kva"], w["wq"], w["wq_t"], w["wk"], w["wv_t"], w["gq"],
               w["gq_t"], w["gk"])
    in_specs = [pl.BlockSpec((tm, d), row), tab, tab, tab, tab_t, tab_t] + [
        _const_spec(a.shape) for a in weights]
    latent_shapes = (jax.ShapeDtypeStruct((depth, t, KV_LORA), F32), jax.ShapeDtypeStruct((t, QK_ROPE), F32))
    latent_specs = (pl.BlockSpec((None, tm, KV_LORA), lambda i: (layer, i, 0)), pl.BlockSpec((tm, QK_ROPE), row))
    if transposed:
        nt = tm // ATT_T
        out_shape = (
            jax.ShapeDtypeStruct((t // ATT_T, SLAB, ATT_T), BF16),
            jax.ShapeDtypeStruct((t, SLAB), BF16),
            jax.ShapeDtypeStruct((t // ATT_T, N_HEADS * VT_ROWS, ATT_T), BF16),
        ) + latent_shapes
        out_specs = (
            pl.BlockSpec((nt, SLAB, ATT_T), lead),
            pl.BlockSpec((tm, SLAB), row),
            pl.BlockSpec((nt, N_HEADS * VT_ROWS, ATT_T), lead),
        ) + latent_specs
    else:
        out_shape = (jax.ShapeDtypeStruct((t, SLAB), BF16),) + latent_shapes
        out_specs = (pl.BlockSpec((tm, SLAB), row),) + latent_specs
    args = [x, c_tab, s1_tab, s2_tab, cos_t, sin_t, *weights]
    aliases = {}
    if ckv_all is not None:
        in_specs.append(pl.BlockSpec(memory_space=pl.ANY))
        aliases = {len(args): len(out_shape) - 2}
        args.append(ckv_all)
    return pl.pallas_call(
        functools.partial(_proj_kernel, transposed=transposed, aliased=ckv_all is not None),
        grid=(t // tm,),
        in_specs=in_specs,
        out_specs=out_specs,
        out_shape=out_shape,
        input_output_aliases=aliases,
        compiler_params=pltpu.CompilerParams(dimension_semantics=("parallel",),
                                             vmem_limit_bytes=VMEM_LIMIT),
        name="proj",
    )(*args)


def _flash_kernel(qt_ref, k_ref, vt_ref, o_ref, m_ref, acc_ref, s_ref, mt_ref):
    t = ATT_T
    g = pl.program_id(1)

    def score_head(slot, tile, masked, hd, qx):
        start = pl.multiple_of(tile * t, t)
        qsl = slice(hd * LANE, (hd + 1) * LANE)
        s = jnp.dot(k_ref[pl.ds(start, t), qsl], qt_ref[qx, qsl, :], preferred_element_type=F32)
        if masked:
            k_chunk = lax.broadcasted_iota(jnp.int32, (t, t), 0) // CHUNK
            q_chunk = lax.broadcasted_iota(jnp.int32, (t, t), 1) // CHUNK
            s = jnp.where(k_chunk <= q_chunk, s, NEG)
        s_ref[slot, hd] = s
        mt_ref[slot, hd:hd + 1, :] = jnp.max(s, axis=0, keepdims=True)

    def value_head(slot, tile, hd):
        vsl = slice(hd * VT_ROWS, (hd + 1) * VT_ROWS)
        m_old = m_ref[hd:hd + 1, :]
        m_new = jnp.maximum(m_old, mt_ref[slot, hd:hd + 1, :])
        alpha = jnp.exp2(m_old - m_new)
        p = jnp.exp2(s_ref[slot, hd] - m_new).astype(BF16)
        pv = jnp.dot(vt_ref[tile, vsl, :], p, preferred_element_type=F32)
        acc_ref[vsl, :] = alpha * acc_ref[vsl, :] + pv
        m_ref[hd:hd + 1, :] = m_new

    def value_phase(slot, tile):
        for hd in range(N_HEADS):
            value_head(slot, tile, hd)

    def both_phases(score_slot, score_tile, value_slot, value_tile, qx, masked=False):
        for hd in range(N_HEADS):
            score_head(score_slot, score_tile, masked, hd, qx)
            value_head(value_slot, value_tile, hd)

    def attend(qx, chained):
        i = 2 * g + qx
        s0, s1 = qx, 1 - qx
        m_ref[...] = jnp.full(m_ref.shape, NEG, F32)
        acc_ref[...] = jnp.zeros(acc_ref.shape, F32)
        if not chained:
            for hd in range(N_HEADS):
                score_head(s0, i, True, hd, qx)

        def tile_pair(first, pending):
            both_phases(s1, first, s0, pending, qx)
            both_phases(s0, first + 1, s1, first, qx)

        n_quads = i // 4

        def quad(p, carry):
            tile_pair(4 * p, jnp.where(p == 0, i, 4 * p - 1))
            tile_pair(4 * p + 2, 4 * p + 1)
            return carry

        lax.fori_loop(0, n_quads, quad, 0)
        done = 4 * n_quads
        pending = jnp.where(n_quads == 0, i, done - 1)
        has_pair = i - done >= 2

        @pl.when(has_pair)
        def _():
            tile_pair(done, pending)

        done = jnp.where(has_pair, done + 2, done)
        pending = jnp.where(has_pair, done - 1, pending)
        if qx == 0:
            both_phases(s1, i + 1, s0, pending, 1, masked=True)
        else:
            both_phases(s1, done, s0, pending, qx)
            value_phase(s1, done)
        outs = []
        for hd in range(N_HEADS):
            base = hd * VT_ROWS
            outs.append(acc_ref[base:base + V_HEAD, :] / acc_ref[base + V_HEAD:base + V_HEAD + 1, :])
        o_ref[qx * t:(qx + 1) * t, :] = jnp.concatenate(outs, axis=0).T.astype(BF16)

    attend(0, False)
    attend(1, True)


def _flash(qt, k, vt, batch, seq):
    assert seq % (2 * ATT_T) == 0 and ATT_T % CHUNK == 0
    nq = seq // ATT_T
    steps = nq // 2
    k3 = k.reshape(batch, seq, SLAB)
    return pl.pallas_call(
        _flash_kernel,
        grid=(batch, steps),
        in_specs=[
            pl.BlockSpec((2, SLAB, ATT_T), lambda b, g: (b * steps + g, 0, 0)),
            pl.BlockSpec((None, seq, SLAB), lambda b, g: (b, 0, 0)),
            pl.BlockSpec((nq, N_HEADS * VT_ROWS, ATT_T), lambda b, g: (b, 0, 0)),
        ],
        out_specs=pl.BlockSpec((2 * ATT_T, ATTN_WIDTH), lambda b, g: (b * steps + g, 0)),
        out_shape=jax.ShapeDtypeStruct((batch * seq, ATTN_WIDTH), BF16),
        scratch_shapes=[pltpu.VMEM((N_HEADS, ATT_T), F32), pltpu.VMEM((N_HEADS * VT_ROWS, ATT_T), F32),
                        pltpu.VMEM((2, N_HEADS, ATT_T, ATT_T), F32), pltpu.VMEM((2, N_HEADS, ATT_T), F32)],
        compiler_params=pltpu.CompilerParams(dimension_semantics=("parallel", "parallel"),
                                             vmem_limit_bytes=VMEM_LIMIT),
        name="flash",
    )(qt, k3, vt)


def _cache_attn_kernel(q_ref, cache_ref, ckr_ref, cnew_ref, krnew_ref, gk_ref, wc_ref, wuk_ref, e_ref,
                       wuv_ref, o_ref, s_ref, *, past, t_new, tk):
    n_tiles = past // tk
    ncol = N_HEADS * t_new
    gk = gk_ref[...]
    qt, qr = [], []
    for hd in range(N_HEADS):
        qg = q_ref[:, hd * LANE:(hd + 1) * LANE].astype(F32) * gk
        qt.append(jnp.dot(qg.astype(BF16), wc_ref[hd], preferred_element_type=F32))
        qr.append(qg[:, :QK_ROPE])
    qt = jnp.concatenate(qt, axis=0).astype(BF16)
    qr = jnp.concatenate(qr, axis=0).astype(BF16)

    nt = (((1,), (1,)), ((), ()))

    def scores_many(tiles):
        cbs = [c.astype(BF16) for c, _ in tiles]
        kns = [jnp.dot(cb, wuk_ref[...], preferred_element_type=F32) for cb in cbs]
        ssqs = [jnp.dot((kn * kn).astype(BF16), e_ref[...], preferred_element_type=F32) for kn in kns]
        invs = [lax.rsqrt((ssq + jnp.sum(kr * kr, axis=-1, keepdims=True)) * (1.0 / QK_HEAD) + EPS)
                for ssq, (_, kr) in zip(ssqs, tiles)]
        ss = [lax.dot_general(cb, qt, nt, preferred_element_type=F32) for cb in cbs]
        ss = [s + lax.dot_general(kr.astype(BF16), qr, nt, preferred_element_type=F32)
              for s, (_, kr) in zip(ss, tiles)]
        return [s * inv for s, inv in zip(ss, invs)]

    def scores(c, kr):
        return scores_many([(c, kr)])[0]

    unroll = 4 if n_tiles % 4 == 0 else 1

    def pass1(j, m):
        starts = [pl.multiple_of((j * unroll + u) * tk, tk) for u in range(unroll)]
        tiles = [(cache_ref[pl.ds(st, tk), :], ckr_ref[:, pl.ds(st, tk)].T) for st in starts]
        for st, s in zip(starts, scores_many(tiles)):
            s_ref[pl.ds(st, tk), :] = s
            m = jnp.maximum(m, jnp.max(s, axis=0, keepdims=True))
        return m

    m = lax.fori_loop(0, n_tiles // unroll, pass1, jnp.full((1, ncol), NEG, F32))
    c_new = cnew_ref[...]
    s_new = scores(c_new, krnew_ref[...])
    m = jnp.maximum(m, jnp.max(s_new, axis=0, keepdims=True))

    tn = (((0,), (0,)), ((), ()))

    def accumulate(s, c):
        p = jnp.exp2(s - m)
        ctx = lax.dot_general(p.astype(BF16), c.astype(BF16), tn, preferred_element_type=F32)
        return jnp.sum(p, axis=0, keepdims=True), ctx

    l_old, ctx_old = accumulate(s_ref[...], cache_ref[...])
    l_new, ctx_new = accumulate(s_new, c_new)
    l, ctx = l_old + l_new, ctx_old + ctx_new

    eye = lax.broadcasted_iota(jnp.int32, (ncol, ncol), 0) == lax.broadcasted_iota(jnp.int32, (ncol, ncol), 1)
    l_col = jnp.sum(jnp.where(eye, jnp.broadcast_to(l, (ncol, ncol)), 0.0), axis=1, keepdims=True)
    ctx = (ctx / l_col).astype(BF16)
    r = jnp.dot(ctx, wuv_ref[...], preferred_element_type=F32)
    col_head = lax.broadcasted_iota(jnp.int32, (t_new, ATTN_WIDTH), 1) // V_HEAD
    out = jnp.zeros((t_new, ATTN_WIDTH), F32)
    for hd in range(N_HEADS):
        out = out + jnp.where(col_head == hd, r[hd * t_new:(hd + 1) * t_new, :], 0.0)
    o_ref[...] = out.astype(BF16)


def _cache_attn(q, cache_ckv, cache_krope_t, layer, c_new, kr_new, w, nseq, t_new, tk):
    past = cache_ckv.shape[2]
    assert past % tk == 0
    ncol = N_HEADS * t_new
    return pl.pallas_call(
        functools.partial(_cache_attn_kernel, past=past, t_new=t_new, tk=tk),
        grid=(nseq,),
        in_specs=[
            pl.BlockSpec((t_new, SLAB), lambda b: (b, 0)),
            pl.BlockSpec((None, None, past, KV_LORA), lambda b: (layer, b, 0, 0)),
            pl.BlockSpec((None, None, QK_ROPE, past), lambda b: (layer, b, 0, 0)),
            pl.BlockSpec((t_new, KV_LORA), lambda b: (b, 0)),
            pl.BlockSpec((t_new, QK_ROPE), lambda b: (b, 0)),
        ] + [_const_spec(a.shape) for a in (w["gk1"], w["wc"], w["wuk"], w["e"], w["wv"])],
        out_specs=pl.BlockSpec((t_new, ATTN_WIDTH), lambda b: (b, 0)),
        out_shape=jax.ShapeDtypeStruct((nseq * t_new, ATTN_WIDTH), BF16),
        scratch_shapes=[pltpu.VMEM((past, ncol), F32)],
        compiler_params=pltpu.CompilerParams(dimension_semantics=("parallel",),
                                             vmem_limit_bytes=VMEM_LIMIT),
        name="cache_attn",
    )(q, cache_ckv, cache_krope_t, c_new, kr_new, w["gk1"], w["wc"], w["wuk"], w["e"], w["wv"])


def _merge_kernel(*refs, groups, rows, tiles_per_seq, pos0, has_hist):
    if has_hist:
        x_ref, a_ref, hist_ref = refs[:3]
        refs = refs[3:]
    else:
        x_ref, a_ref = refs[:2]
        hist_ref = None
        refs = refs[2:]
    (gmix_ref, w2_ref, wpool_ref, pscale_ref, wao_ref, wpo_ref, wo_ref, gmlp_ref, wup_ref, wdown_ref,
     y_ref, pool_ref, ext_ref) = refs
    tm = groups * rows
    tile = pl.program_id(0) % tiles_per_seq

    x = x_ref[...]
    h = _rms(x, gmix_ref[...]).astype(BF16)
    z = jnp.dot(h, w2_ref[...], preferred_element_type=F32)
    d = x.shape[1]
    p = z[:, :POOL_WIDTH]
    p3 = p.reshape(groups, rows, POOL_WIDTH)

    if has_hist:
        ext_ref[:, :POOL_PAD, :] = hist_ref[...]
    else:
        @pl.when(tile == 0)
        def _():
            ext_ref[:, :POOL_PAD, :] = jnp.zeros((groups, POOL_PAD, POOL_WIDTH), F32)

        @pl.when(tile != 0)
        def _():
            ext_ref[:, :POOL_PAD, :] = ext_ref[:, rows:rows + POOL_PAD, :]
    ext_ref[:, POOL_PAD:, :] = p3
    pool_ref[...] = p3[:, rows - POOL_PAD:, :]

    pos = pos0 + tile * rows + lax.broadcasted_iota(jnp.int32, (groups, rows, POOL_GROUP), 1)
    us = []
    for g, win in enumerate(POOL_WINDOWS):
        sl = slice(g * POOL_GROUP, (g + 1) * POOL_GROUP)
        tot = p3[:, :, sl]
        for j in range(1, win):
            tot = tot + ext_ref[:, pl.ds(POOL_PAD - j, rows), sl]
        cnt = jnp.minimum(win, pos + 1).astype(F32)
        pooled = (tot / cnt - p3[:, :, sl]).reshape(tm, POOL_GROUP)
        u = jnp.dot(pooled.astype(BF16), wpool_ref[g], preferred_element_type=F32)
        us.append((u * pscale_ref[:, sl]).astype(BF16))
    u = jnp.concatenate(us, axis=1)

    branch_a = jnp.dot(a_ref[...], wao_ref[...], preferred_element_type=F32)
    branch_b = jnp.dot(u, wpo_ref[...], preferred_element_type=F32)
    gate_a = 1.0 / (1.0 + jnp.exp(-z[:, POOL_WIDTH:POOL_WIDTH + d]))
    gate_b = 1.0 / (1.0 + jnp.exp(-z[:, POOL_WIDTH + d:]))
    mix = (gate_a * branch_a + gate_b * branch_b).astype(BF16)
    x = x + jnp.dot(mix, wo_ref[...], preferred_element_type=F32)
    hm = _rms(x, gmlp_ref[...]).astype(BF16)
    up = jnp.dot(hm, wup_ref[...], preferred_element_type=F32)
    act = jnp.square(jnp.maximum(up, 0.0)).astype(BF16)
    y_ref[...] = x + jnp.dot(act, wdown_ref[...], preferred_element_type=F32)


def _merge(x, attn, hist, w, nseq, seq, tm, pos0):
    t, d = x.shape
    assert t % tm == 0
    if hist is None:
        assert seq % tm == 0
        groups, rows, tiles_per_seq = 1, tm, seq // tm
    else:
        assert tm % seq == 0 and seq >= POOL_PAD
        groups, rows, tiles_per_seq = tm // seq, seq, 1
    assert rows % 8 == 0 and rows >= POOL_PAD
    row = lambda i: (i, 0)
    seq_block = lambda i: (i // tiles_per_seq, 0, 0)
    in_specs = [pl.BlockSpec((tm, d), row), pl.BlockSpec((tm, ATTN_WIDTH), row)]
    args = [x, attn]
    if hist is not None:
        in_specs.append(pl.BlockSpec((groups, POOL_PAD, POOL_WIDTH), seq_block))
        args.append(hist)
    weights = (w["g_mix"], w["w2"], w["w_pool"], w["pool_scale"], w["w_attn_out"], w["w_pool_out"],
               w["w_o"], w["g_mlp"], w["w_up"], w["w_down"])
    in_specs += [_const_spec(a.shape) for a in weights]
    return pl.pallas_call(
        functools.partial(_merge_kernel, groups=groups, rows=rows, tiles_per_seq=tiles_per_seq,
                          pos0=pos0, has_hist=hist is not None),
        grid=(t // tm,),
        in_specs=in_specs,
        out_specs=(pl.BlockSpec((tm, d), row),
                   pl.BlockSpec((groups, POOL_PAD, POOL_WIDTH), seq_block)),
        out_shape=(jax.ShapeDtypeStruct((t, d), F32),
                   jax.ShapeDtypeStruct((nseq, POOL_PAD, POOL_WIDTH), F32)),
        scratch_shapes=[pltpu.VMEM((groups, POOL_PAD + rows, POOL_WIDTH), F32)],
        compiler_params=pltpu.CompilerParams(dimension_semantics=("arbitrary",),
                                             vmem_limit_bytes=VMEM_LIMIT),
        name="merge",
    )(*args, *weights)


def _rope_tables(pos):
    half = QK_ROPE // 2
    inv = jnp.power(ROPE_THETA, -jnp.arange(half, dtype=F32) / half)
    ang = pos[:, None] * inv[None, :]
    cos, sin = jnp.cos(ang), jnp.sin(ang)
    n = pos.shape[0]
    one = jnp.ones((n, LANE - QK_ROPE), F32)
    zero = jnp.zeros((n, LANE - half), F32)
    c = jnp.concatenate([cos, cos, one], axis=1)
    s1 = jnp.concatenate([jnp.zeros((n, half), F32), sin, jnp.zeros((n, LANE - QK_ROPE), F32)], axis=1)
    s2 = jnp.concatenate([-sin, zero], axis=1)
    return c, s1, s2, cos.T, sin.T


def _slab(rope_part, nope_part):
    pad = jnp.zeros(rope_part.shape[:-1] + (LANE - QK_HEAD,), rope_part.dtype)
    s = jnp.concatenate([rope_part, nope_part, pad], axis=-1)
    return s.reshape(s.shape[:-2] + (N_HEADS * LANE,))


def _layer_weights(l, t_new, g_mix, w_in, g_qa, g_kva, w_uq, w_ukv, g_q, g_k, w_attn_out, w_pool,
                   pool_scale, w_pool_out, w_o, g_mlp, w_up, w_down):
    off_kv, off_kr = Q_LORA, Q_LORA + KV_LORA
    off_p = off_kr + QK_ROPE
    wi = w_in[l]
    d = wi.shape[0]
    w1 = jnp.concatenate([wi[:, :off_p], jnp.zeros((d, LANE - QK_ROPE), F32)], axis=1)
    uq = w_uq[l]
    ukv = w_ukv[l]
    k_nope, v = ukv[..., :QK_NOPE], ukv[..., QK_NOPE:]
    zr = jnp.zeros(k_nope.shape[:-1] + (QK_ROPE,), F32)
    gq, gk = g_q[l], g_k[l]
    slab_gain = lambda g: jnp.concatenate([g[QK_NOPE:], g[:QK_NOPE], jnp.zeros((LANE - QK_HEAD,), F32)])[None]
    wc = jnp.zeros((N_HEADS, LANE, KV_LORA), F32).at[:, QK_ROPE:QK_HEAD, :].set(k_nope.transpose(1, 2, 0))
    ncol = N_HEADS * t_new
    e = (jnp.arange(N_HEADS * QK_NOPE)[:, None] // QK_NOPE == jnp.arange(ncol)[None, :] // t_new)
    return {
        "g_mix": g_mix[l][None],
        "w1": w1.astype(BF16),
        "w2": wi[:, off_p:].astype(BF16),
        "g_qa": g_qa[l][None],
        "g_kva": g_kva[l][None],
        "wq": _slab(uq[..., QK_NOPE:], uq[..., :QK_NOPE]).astype(BF16),
        "wq_t": _slab(uq[..., QK_NOPE:], uq[..., :QK_NOPE]).T.astype(BF16),
        "wk": _slab(zr, k_nope).astype(BF16),
        "wv": v.reshape(KV_LORA, ATTN_WIDTH).astype(BF16),
        "wv_t": v.reshape(KV_LORA, ATTN_WIDTH).T.astype(BF16),
        "wuk": k_nope.reshape(KV_LORA, N_HEADS * QK_NOPE).astype(BF16),
        "wc": wc.astype(BF16),
        "e": e.astype(BF16),
        "gq": slab_gain(gq) * (SM_SCALE * LOG2E),
        "gq_t": slab_gain(gq).T * (SM_SCALE * LOG2E),
        "gk": slab_gain(gk),
        "gk1": slab_gain(gk),
        "w_attn_out": w_attn_out[l].astype(BF16),
        "w_pool": w_pool[l].astype(BF16),
        "pool_scale": pool_scale[l][None],
        "w_pool_out": w_pool_out[l].astype(BF16),
        "w_o": w_o[l].astype(BF16),
        "g_mlp": g_mlp[l][None],
        "w_up": w_up[l].astype(BF16),
        "w_down": w_down[l].astype(BF16),
    }


def kernel(x_prompt, x_sample, cache_ckv, cache_krope, state_pool, g_mix, w_in, g_qa, g_kva, w_uq, w_ukv,
           g_q, g_k, w_attn_out, w_pool, pool_scale, w_pool_out, w_o, g_mlp, w_up, w_down):
    batch, seq, d = x_prompt.shape
    nseq, t_new, _ = x_sample.shape
    depth = g_mix.shape[0]
    past = cache_ckv.shape[2]

    tabs_p = _rope_tables(jnp.arange(seq, dtype=F32))
    proj_tm_s = min(PROJ_TM, nseq * t_new)
    reps = proj_tm_s // t_new
    tabs_s = _rope_tables(past + jnp.arange(t_new, dtype=F32))
    tabs_s = tuple(jnp.tile(a, (reps, 1)) for a in tabs_s[:3]) + tuple(jnp.tile(a, (1, reps)) for a in tabs_s[3:])
    hist = jnp.pad(state_pool, ((0, 0), (0, 0), (POOL_PAD - POOL_HIST, 0), (0, 0)))
    krope_t = jnp.swapaxes(cache_krope, 2, 3)

    xp = x_prompt.reshape(batch * seq, d)
    xs = x_sample.reshape(nseq * t_new, d)
    outs = {k: [] for k in ("kr_p", "pool_p", "kr_s", "pool_s")}
    ckv_p = ckv_s = None
    for l in range(depth):
        w = _layer_weights(l, t_new, g_mix, w_in, g_qa, g_kva, w_uq, w_ukv, g_q, g_k, w_attn_out, w_pool,
                           pool_scale, w_pool_out, w_o, g_mlp, w_up, w_down)
        qt, k, vt, ckv_p, kr = _proj(xp, tabs_p, seq // PROJ_TM, w, PROJ_TM, True, l, depth, ckv_p)
        attn = _flash(qt, k, vt, batch, seq)
        xp, pool = _merge(xp, attn, None, w, batch, seq, MERGE_TM, 0)
        outs["kr_p"].append(kr.reshape(batch, seq, QK_ROPE))
        outs["pool_p"].append(pool[:, POOL_PAD - POOL_HIST:])

        q, ckv_s, kr = _proj(xs, tabs_s, 1, w, proj_tm_s, False, l, depth, ckv_s)
        attn = _cache_attn(q, cache_ckv, krope_t, l, ckv_s[l], kr, w, nseq, t_new, CACHE_TK)
        xs, pool = _merge(xs, attn, hist[l], w, nseq, t_new, min(MERGE_TM, nseq * t_new), past)
        outs["kr_s"].append(kr.reshape(nseq, t_new, QK_ROPE))
        outs["pool_s"].append(pool[:, POOL_PAD - POOL_HIST:])

    return (xp.reshape(batch, seq, d), xs.reshape(nseq, t_new, d),
            ckv_p.reshape(depth, batch, seq, KV_LORA), jnp.stack(outs["kr_p"]), jnp.stack(outs["pool_p"]),
            ckv_s.reshape(depth, nseq, t_new, KV_LORA), jnp.stack(outs["kr_s"]), jnp.stack(outs["pool_s"]))
```

```python
import functools

import jax
import jax.numpy as jnp
from jax import lax
from jax.experimental import pallas as pl
from jax.experimental.pallas import tpu as pltpu

F32 = jnp.float32
BF16 = jnp.bfloat16

LANE = 128
CHUNK = 64
N_HEADS = 8
QK_NOPE = 64
QK_ROPE = 32
QK_HEAD = QK_NOPE + QK_ROPE
V_HEAD = 64
Q_LORA = 384
KV_LORA = 256
ATTN_WIDTH = N_HEADS * V_HEAD
VT_ROWS = V_HEAD + 16
POOL_WINDOWS = (2, 4, 8, 16)
POOL_GROUP = 128
POOL_WIDTH = len(POOL_WINDOWS) * POOL_GROUP
POOL_HIST = max(POOL_WINDOWS) - 1
POOL_PAD = POOL_HIST + 1
ROPE_THETA = 10000.0
EPS = 1e-6
SM_SCALE = QK_HEAD ** -0.5
LOG2E = 1.4426950408889634
SLAB = N_HEADS * LANE
NEG = -1e30

PROJ_TM = 1024
ATT_T = 256
MERGE_TM = 1024
CACHE_TK = 512
VMEM_LIMIT = 62 * 1024 * 1024


def _const_spec(shape):
    nd = len(shape)
    return pl.BlockSpec(shape, lambda *_: (0,) * nd, pipeline_mode=pl.Buffered(1))


def _rms(x, g):
    ms = jnp.mean(x * x, axis=-1, keepdims=True)
    return x * lax.rsqrt(ms + EPS) * g


def _rope_slab(x, c, s1, s2):
    return x * c + pltpu.roll(x, 16, 1) * s1 + pltpu.roll(x, LANE - 16, 1) * s2


def _head_norm(x, g):
    ss = jnp.sum(x * x, axis=-1, keepdims=True)
    return x * lax.rsqrt(ss * (1.0 / QK_HEAD) + EPS) * g


def _proj_kernel(x_ref, c_ref, s1_ref, s2_ref, ct_ref, st_ref, gmix_ref, w1_ref, gqa_ref, gkva_ref, wq_ref,
                 wqt_ref, wk_ref, wvt_ref, gq_ref, gqt_ref, gk_ref, *out_refs, transposed, aliased):
    if aliased:
        out_refs = out_refs[1:]
    if transposed:
        q_ref, k_ref, v_ref, ckv_ref, kr_ref = out_refs
    else:
        q_ref, ckv_ref, kr_ref = out_refs
    half = QK_ROPE // 2
    for t in range(x_ref.shape[0] // ATT_T):
        rows = slice(t * ATT_T, (t + 1) * ATT_T)
        h = _rms(x_ref[rows, :], gmix_ref[...]).astype(BF16)
        z = jnp.dot(h, w1_ref[...], preferred_element_type=F32)
        cq = _rms(z[:, :Q_LORA], gqa_ref[...]).astype(BF16)
        ckv = _rms(z[:, Q_LORA:Q_LORA + KV_LORA], gkva_ref[...])
        ckv_ref[rows, :] = ckv
        c, s1, s2 = c_ref[rows, :], s1_ref[rows, :], s2_ref[rows, :]
        kr = _rope_slab(z[:, Q_LORA + KV_LORA:], c, s1, s2)
        kr_ref[rows, :] = kr[:, :QK_ROPE]
        if not transposed:
            q = jnp.dot(cq, wq_ref[...], preferred_element_type=F32)
            for hd in range(N_HEADS):
                sl = slice(hd * LANE, (hd + 1) * LANE)
                q_ref[rows, sl] = _head_norm(_rope_slab(q[:, sl], c, s1, s2), gq_ref[...]).astype(BF16)
            continue
        ckv_b = ckv.astype(BF16)
        kn = jnp.dot(ckv_b, wk_ref[...], preferred_element_type=F32)
        nt = (((1,), (1,)), ((), ()))
        vt = lax.dot_general(wvt_ref[...], ckv_b, nt, preferred_element_type=F32).astype(BF16)
        qt_all = lax.dot_general(wqt_ref[...], cq, nt, preferred_element_type=F32)
        ones = jnp.ones((VT_ROWS - V_HEAD, ATT_T), BF16)
        cos, sin = ct_ref[:, rows], st_ref[:, rows]
        for hd in range(N_HEADS):
            sl = slice(hd * LANE, (hd + 1) * LANE)
            v_ref[t, hd * VT_ROWS:hd * VT_ROWS + V_HEAD, :] = vt[hd * V_HEAD:(hd + 1) * V_HEAD, :]
            v_ref[t, hd * VT_ROWS + V_HEAD:(hd + 1) * VT_ROWS, :] = ones
            qt = qt_all[sl, :]
            x1, x2 = qt[:half], qt[half:QK_ROPE]
            qt = jnp.concatenate([x1 * cos - x2 * sin, x1 * sin + x2 * cos, qt[QK_ROPE:]], axis=0)
            ss = jnp.sum(qt * qt, axis=0, keepdims=True)
            q_ref[t, sl, :] = (qt * lax.rsqrt(ss * (1.0 / QK_HEAD) + EPS) * gqt_ref[...]).astype(BF16)
            k_ref[hd, rows, :] = _head_norm(kn[:, sl] + kr, gk_ref[...]).astype(BF16)


def _proj(x, tabs, tab_tiles, w, tm, transposed, layer, depth, ckv_all):
    t = x.shape[0]
    d = x.shape[1]
    assert t % tm == 0 and tm % ATT_T == 0
    c_tab, s1_tab, s2_tab, cos_t, sin_t = tabs
    row = lambda i: (i, 0)
    lead = lambda i: (i, 0, 0)
    tab = pl.BlockSpec((tm, LANE), lambda i: (i % tab_tiles, 0))
    tab_t = pl.BlockSpec((QK_ROPE // 2, tm), lambda i: (0, i % tab_tiles))
    weights = (w["g_mix"], w["w1"], w["g_qa"], w["g_kva"], w["wq"], w["wq_t"], w["wk"], w["wv_t"], w["gq"],
               w["gq_t"], w["gk"])
    in_specs = [pl.BlockSpec((tm, d), row), tab, tab, tab, tab_t, tab_t] + [
        _const_spec(a.shape) for a in weights]
    latent_shapes = (jax.ShapeDtypeStruct((depth, t, KV_LORA), F32), jax.ShapeDtypeStruct((t, QK_ROPE), F32))
    latent_specs = (pl.BlockSpec((None, tm, KV_LORA), lambda i: (layer, i, 0)), pl.BlockSpec((tm, QK_ROPE), row))
    if transposed:
        nt = tm // ATT_T
        out_shape = (
            jax.ShapeDtypeStruct((t // ATT_T, SLAB, ATT_T), BF16),
            jax.ShapeDtypeStruct((N_HEADS, t, LANE), BF16),
            jax.ShapeDtypeStruct((t // ATT_T, N_HEADS * VT_ROWS, ATT_T), BF16),
        ) + latent_shapes
        out_specs = (
            pl.BlockSpec((nt, SLAB, ATT_T), lead),
            pl.BlockSpec((N_HEADS, tm, LANE), lambda i: (0, i, 0)),
            pl.BlockSpec((nt, N_HEADS * VT_ROWS, ATT_T), lead),
        ) + latent_specs
    else:
        out_shape = (jax.ShapeDtypeStruct((t, SLAB), BF16),) + latent_shapes
        out_specs = (pl.BlockSpec((tm, SLAB), row),) + latent_specs
    args = [x, c_tab, s1_tab, s2_tab, cos_t, sin_t, *weights]
    aliases = {}
    if ckv_all is not None:
        in_specs.append(pl.BlockSpec(memory_space=pl.ANY))
        aliases = {len(args): len(out_shape) - 2}
        args.append(ckv_all)
    return pl.pallas_call(
        functools.partial(_proj_kernel, transposed=transposed, aliased=ckv_all is not None),
        grid=(t // tm,),
        in_specs=in_specs,
        out_specs=out_specs,
        out_shape=out_shape,
        input_output_aliases=aliases,
        compiler_params=pltpu.CompilerParams(dimension_semantics=("parallel",),
                                             vmem_limit_bytes=VMEM_LIMIT),
        name="proj",
    )(*args)


def _flash_kernel(qt_ref, k_ref, vt_ref, o_ref, m_ref, acc_ref, s_ref, mt_ref):
    t = ATT_T
    g = pl.program_id(1)

    def score_head(slot, tile, masked, hd, qx):
        start = pl.multiple_of(tile * t, t)
        qsl = slice(hd * LANE, (hd + 1) * LANE)
        s = jnp.dot(k_ref[hd, pl.ds(start, t), :], qt_ref[qx, qsl, :], preferred_element_type=F32)
        if masked:
            k_chunk = lax.broadcasted_iota(jnp.int32, (t, t), 0) // CHUNK
            q_chunk = lax.broadcasted_iota(jnp.int32, (t, t), 1) // CHUNK
            s = jnp.where(k_chunk <= q_chunk, s, NEG)
        s_ref[slot, hd] = s
        mt_ref[slot, hd:hd + 1, :] = jnp.max(s, axis=0, keepdims=True)

    def value_head(slot, tile, hd):
        vsl = slice(hd * VT_ROWS, (hd + 1) * VT_ROWS)
        m_old = m_ref[hd:hd + 1, :]
        m_new = jnp.maximum(m_old, mt_ref[slot, hd:hd + 1, :])
        alpha = jnp.exp2(m_old - m_new)
        p = jnp.exp2(s_ref[slot, hd] - m_new).astype(BF16)
        pv = jnp.dot(vt_ref[tile, vsl, :], p, preferred_element_type=F32)
        acc_ref[vsl, :] = alpha * acc_ref[vsl, :] + pv
        m_ref[hd:hd + 1, :] = m_new

    def value_phase(slot, tile):
        for hd in range(N_HEADS):
            value_head(slot, tile, hd)

    def both_phases(score_slot, score_tile, value_slot, value_tile, qx, masked=False):
        for hd in range(N_HEADS):
            score_head(score_slot, score_tile, masked, hd, qx)
            value_head(value_slot, value_tile, hd)

    def attend(qx, chained):
        i = 2 * g + qx
        s0, s1 = qx, 1 - qx
        m_ref[...] = jnp.full(m_ref.shape, NEG, F32)
        acc_ref[...] = jnp.zeros(acc_ref.shape, F32)
        if not chained:
            for hd in range(N_HEADS):
                score_head(s0, i, True, hd, qx)

        def tile_pair(first, pending):
            both_phases(s1, first, s0, pending, qx)
            both_phases(s0, first + 1, s1, first, qx)

        n_quads = i // 4

        def quad(p, carry):
            tile_pair(4 * p, jnp.where(p == 0, i, 4 * p - 1))
            tile_pair(4 * p + 2, 4 * p + 1)
            return carry

        lax.fori_loop(0, n_quads, quad, 0)
        done = 4 * n_quads
        pending = jnp.where(n_quads == 0, i, done - 1)
        has_pair = i - done >= 2

        @pl.when(has_pair)
        def _():
            tile_pair(done, pending)

        done = jnp.where(has_pair, done + 2, done)
        pending = jnp.where(has_pair, done - 1, pending)
        if qx == 0:
            both_phases(s1, i + 1, s0, pending, 1, masked=True)
        else:
            both_phases(s1, done, s0, pending, qx)
            value_phase(s1, done)
        outs = []
        for hd in range(N_HEADS):
            base = hd * VT_ROWS
            outs.append(acc_ref[base:base + V_HEAD, :] / acc_ref[base + V_HEAD:base + V_HEAD + 1, :])
        o_ref[qx * t:(qx + 1) * t, :] = jnp.concatenate(outs, axis=0).T.astype(BF16)

    attend(0, False)
    attend(1, True)


def _flash(qt, k, vt, batch, seq):
    assert seq % (2 * ATT_T) == 0 and ATT_T % CHUNK == 0
    nq = seq // ATT_T
    steps = nq // 2
    return pl.pallas_call(
        _flash_kernel,
        grid=(batch, steps),
        in_specs=[
            pl.BlockSpec((2, SLAB, ATT_T), lambda b, g: (b * steps + g, 0, 0)),
            pl.BlockSpec((N_HEADS, seq, LANE), lambda b, g: (0, b, 0)),
            pl.BlockSpec((nq, N_HEADS * VT_ROWS, ATT_T), lambda b, g: (b, 0, 0)),
        ],
        out_specs=pl.BlockSpec((2 * ATT_T, ATTN_WIDTH), lambda b, g: (b * steps + g, 0)),
        out_shape=jax.ShapeDtypeStruct((batch * seq, ATTN_WIDTH), BF16),
        scratch_shapes=[pltpu.VMEM((N_HEADS, ATT_T), F32), pltpu.VMEM((N_HEADS * VT_ROWS, ATT_T), F32),
                        pltpu.VMEM((2, N_HEADS, ATT_T, ATT_T), F32), pltpu.VMEM((2, N_HEADS, ATT_T), F32)],
        compiler_params=pltpu.CompilerParams(dimension_semantics=("parallel", "parallel"),
                                             vmem_limit_bytes=VMEM_LIMIT),
        name="flash",
    )(qt, k, vt)


def _cache_attn_kernel(q_ref, cache_ref, ckr_ref, cnew_ref, krnew_ref, gk_ref, wc_ref, wuk_ref, e_ref,
                       wuv_ref, o_ref, s_ref, *, past, t_new, tk):
    n_tiles = past // tk
    ncol = N_HEADS * t_new
    gk = gk_ref[...]
    qt, qr = [], []
    for hd in range(N_HEADS):
        qg = q_ref[:, hd * LANE:(hd + 1) * LANE].astype(F32) * gk
        qt.append(jnp.dot(qg.astype(BF16), wc_ref[hd], preferred_element_type=F32))
        qr.append(qg[:, :QK_ROPE])
    qt = jnp.concatenate(qt, axis=0).astype(BF16)
    qr = jnp.concatenate(qr, axis=0).astype(BF16)

    nt = (((1,), (1,)), ((), ()))

    def scores_many(tiles):
        cbs = [c.astype(BF16) for c, _ in tiles]
        kns = [jnp.dot(cb, wuk_ref[...], preferred_element_type=F32) for cb in cbs]
        ssqs = [jnp.dot((kn * kn).astype(BF16), e_ref[...], preferred_element_type=F32) for kn in kns]
        invs = [lax.rsqrt((ssq + jnp.sum(kr * kr, axis=-1, keepdims=True)) * (1.0 / QK_HEAD) + EPS)
                for ssq, (_, kr) in zip(ssqs, tiles)]
        ss = [lax.dot_general(cb, qt, nt, preferred_element_type=F32) for cb in cbs]
        ss = [s + lax.dot_general(kr.astype(BF16), qr, nt, preferred_element_type=F32)
              for s, (_, kr) in zip(ss, tiles)]
        return [s * inv for s, inv in zip(ss, invs)]

    def scores(c, kr):
        return scores_many([(c, kr)])[0]

    unroll = 8 if n_tiles % 8 == 0 else 1

    def pass1(j, m):
        starts = [pl.multiple_of((j * unroll + u) * tk, tk) for u in range(unroll)]
        tiles = [(cache_ref[pl.ds(st, tk), :], ckr_ref[:, pl.ds(st, tk)].T) for st in starts]
        for st, s in zip(starts, scores_many(tiles)):
            s_ref[pl.ds(st, tk), :] = s
            m = jnp.maximum(m, jnp.max(s, axis=0, keepdims=True))
        return m

    m = lax.fori_loop(0, n_tiles // unroll, pass1, jnp.full((1, ncol), NEG, F32))
    c_new = cnew_ref[...]
    s_new = scores(c_new, krnew_ref[...])
    m = jnp.maximum(m, jnp.max(s_new, axis=0, keepdims=True))

    tn = (((0,), (0,)), ((), ()))

    def accumulate(s, c):
        p = jnp.exp2(s - m)
        ctx = lax.dot_general(p.astype(BF16), c.astype(BF16), tn, preferred_element_type=F32)
        return jnp.sum(p, axis=0, keepdims=True), ctx

    l_old, ctx_old = accumulate(s_ref[...], cache_ref[...])
    l_new, ctx_new = accumulate(s_new, c_new)
    l, ctx = l_old + l_new, ctx_old + ctx_new

    eye = lax.broadcasted_iota(jnp.int32, (ncol, ncol), 0) == lax.broadcasted_iota(jnp.int32, (ncol, ncol), 1)
    l_col = jnp.sum(jnp.where(eye, jnp.broadcast_to(l, (ncol, ncol)), 0.0), axis=1, keepdims=True)
    ctx = (ctx / l_col).astype(BF16)
    r = jnp.dot(ctx, wuv_ref[...], preferred_element_type=F32)
    col_head = lax.broadcasted_iota(jnp.int32, (t_new, ATTN_WIDTH), 1) // V_HEAD
    out = jnp.zeros((t_new, ATTN_WIDTH), F32)
    for hd in range(N_HEADS):
        out = out + jnp.where(col_head == hd, r[hd * t_new:(hd + 1) * t_new, :], 0.0)
    o_ref[...] = out.astype(BF16)


def _cache_attn(q, cache_ckv, cache_krope_t, layer, c_new, kr_new, w, nseq, t_new, tk):
    past = cache_ckv.shape[2]
    assert past % tk == 0
    ncol = N_HEADS * t_new
    return pl.pallas_call(
        functools.partial(_cache_attn_kernel, past=past, t_new=t_new, tk=tk),
        grid=(nseq,),
        in_specs=[
            pl.BlockSpec((t_new, SLAB), lambda b: (b, 0)),
            pl.BlockSpec((None, None, past, KV_LORA), lambda b: (layer, b, 0, 0)),
            pl.BlockSpec((None, None, QK_ROPE, past), lambda b: (layer, b, 0, 0)),
            pl.BlockSpec((t_new, KV_LORA), lambda b: (b, 0)),
            pl.BlockSpec((t_new, QK_ROPE), lambda b: (b, 0)),
        ] + [_const_spec(a.shape) for a in (w["gk1"], w["wc"], w["wuk"], w["e"], w["wv"])],
        out_specs=pl.BlockSpec((t_new, ATTN_WIDTH), lambda b: (b, 0)),
        out_shape=jax.ShapeDtypeStruct((nseq * t_new, ATTN_WIDTH), BF16),
        scratch_shapes=[pltpu.VMEM((past, ncol), F32)],
        compiler_params=pltpu.CompilerParams(dimension_semantics=("parallel",),
                                             vmem_limit_bytes=VMEM_LIMIT),
        name="cache_attn",
    )(q, cache_ckv, cache_krope_t, c_new, kr_new, w["gk1"], w["wc"], w["wuk"], w["e"], w["wv"])


def _merge_kernel(*refs, groups, rows, tiles_per_seq, pos0, has_hist):
    if has_hist:
        x_ref, a_ref, hist_ref = refs[:3]
        refs = refs[3:]
    else:
        x_ref, a_ref = refs[:2]
        hist_ref = None
        refs = refs[2:]
    (gmix_ref, w2_ref, wpool_ref, pscale_ref, wao_ref, wpo_ref, wo_ref, gmlp_ref, wup_ref, wdown_ref,
     y_ref, pool_ref, ext_ref) = refs
    tm = groups * rows
    tile = pl.program_id(0) % tiles_per_seq

    x = x_ref[...]
    h = _rms(x, gmix_ref[...]).astype(BF16)
    z = jnp.dot(h, w2_ref[...], preferred_element_type=F32)
    d = x.shape[1]
    p = z[:, :POOL_WIDTH]
    p3 = p.reshape(groups, rows, POOL_WIDTH)

    group_slices = [slice(g * POOL_GROUP, (g + 1) * POOL_GROUP) for g in range(len(POOL_WINDOWS))]
    for g, sl in enumerate(group_slices):
        if has_hist:
            ext_ref[g, :, :POOL_PAD, :] = hist_ref[:, :, sl]
        else:
            @pl.when(tile == 0)
            def _():
                ext_ref[g, :, :POOL_PAD, :] = jnp.zeros((groups, POOL_PAD, POOL_GROUP), F32)

            @pl.when(tile != 0)
            def _():
                ext_ref[g, :, :POOL_PAD, :] = ext_ref[g, :, rows:rows + POOL_PAD, :]
        ext_ref[g, :, POOL_PAD:, :] = p3[:, :, sl]
    pool_ref[...] = p3[:, rows - POOL_PAD:, :]

    pos = pos0 + tile * rows + lax.broadcasted_iota(jnp.int32, (groups, rows, POOL_GROUP), 1)
    us = []
    for g, win in enumerate(POOL_WINDOWS):
        sl = slice(g * POOL_GROUP, (g + 1) * POOL_GROUP)
        tot = p3[:, :, sl]
        for j in range(1, win):
            tot = tot + ext_ref[g, :, pl.ds(POOL_PAD - j, rows), :]
        cnt = jnp.minimum(win, pos + 1).astype(F32)
        pooled = (tot / cnt - p3[:, :, sl]).reshape(tm, POOL_GROUP)
        u = jnp.dot(pooled.astype(BF16), wpool_ref[g], preferred_element_type=F32)
        us.append((u * pscale_ref[:, sl]).astype(BF16))
    u = jnp.concatenate(us, axis=1)

    branch_a = jnp.dot(a_ref[...], wao_ref[...], preferred_element_type=F32)
    branch_b = jnp.dot(u, wpo_ref[...], preferred_element_type=F32)
    gate_a = 1.0 / (1.0 + jnp.exp(-z[:, POOL_WIDTH:POOL_WIDTH + d]))
    gate_b = 1.0 / (1.0 + jnp.exp(-z[:, POOL_WIDTH + d:]))
    mix = (gate_a * branch_a + gate_b * branch_b).astype(BF16)
    x = x + jnp.dot(mix, wo_ref[...], preferred_element_type=F32)
    hm = _rms(x, gmlp_ref[...]).astype(BF16)
    up = jnp.dot(hm, wup_ref[...], preferred_element_type=F32)
    act = jnp.square(jnp.maximum(up, 0.0)).astype(BF16)
    y_ref[...] = x + jnp.dot(act, wdown_ref[...], preferred_element_type=F32)


def _merge(x, attn, hist, w, nseq, seq, tm, pos0):
    t, d = x.shape
    assert t % tm == 0
    if hist is None:
        assert seq % tm == 0
        groups, rows, tiles_per_seq = 1, tm, seq // tm
    else:
        assert tm % seq == 0 and seq >= POOL_PAD
        groups, rows, tiles_per_seq = tm // seq, seq, 1
    assert rows % 8 == 0 and rows >= POOL_PAD
    row = lambda i: (i, 0)
    seq_block = lambda i: (i // tiles_per_seq, 0, 0)
    in_specs = [pl.BlockSpec((tm, d), row), pl.BlockSpec((tm, ATTN_WIDTH), row)]
    args = [x, attn]
    if hist is not None:
        in_specs.append(pl.BlockSpec((groups, POOL_PAD, POOL_WIDTH), seq_block))
        args.append(hist)
    weights = (w["g_mix"], w["w2"], w["w_pool"], w["pool_scale"], w["w_attn_out"], w["w_pool_out"],
               w["w_o"], w["g_mlp"], w["w_up"], w["w_down"])
    in_specs += [_const_spec(a.shape) for a in weights]
    return pl.pallas_call(
        functools.partial(_merge_kernel, groups=groups, rows=rows, tiles_per_seq=tiles_per_seq,
                          pos0=pos0, has_hist=hist is not None),
        grid=(t // tm,),
        in_specs=in_specs,
        out_specs=(pl.BlockSpec((tm, d), row),
                   pl.BlockSpec((groups, POOL_PAD, POOL_WIDTH), seq_block)),
        out_shape=(jax.ShapeDtypeStruct((t, d), F32),
                   jax.ShapeDtypeStruct((nseq, POOL_PAD, POOL_WIDTH), F32)),
        scratch_shapes=[pltpu.VMEM((len(POOL_WINDOWS), groups, POOL_PAD + rows, POOL_GROUP), F32)],
        compiler_params=pltpu.CompilerParams(dimension_semantics=("arbitrary",),
                                             vmem_limit_bytes=VMEM_LIMIT),
        name="merge",
    )(*args, *weights)


def _rope_tables(pos):
    half = QK_ROPE // 2
    inv = jnp.power(ROPE_THETA, -jnp.arange(half, dtype=F32) / half)
    ang = pos[:, None] * inv[None, :]
    cos, sin = jnp.cos(ang), jnp.sin(ang)
    n = pos.shape[0]
    one = jnp.ones((n, LANE - QK_ROPE), F32)
    zero = jnp.zeros((n, LANE - half), F32)
    c = jnp.concatenate([cos, cos, one], axis=1)
    s1 = jnp.concatenate([jnp.zeros((n, half), F32), sin, jnp.zeros((n, LANE - QK_ROPE), F32)], axis=1)
    s2 = jnp.concatenate([-sin, zero], axis=1)
    return c, s1, s2, cos.T, sin.T


def _slab(rope_part, nope_part):
    pad = jnp.zeros(rope_part.shape[:-1] + (LANE - QK_HEAD,), rope_part.dtype)
    s = jnp.concatenate([rope_part, nope_part, pad], axis=-1)
    return s.reshape(s.shape[:-2] + (N_HEADS * LANE,))


def _layer_weights(l, t_new, g_mix, w_in, g_qa, g_kva, w_uq, w_ukv, g_q, g_k, w_attn_out, w_pool,
                   pool_scale, w_pool_out, w_o, g_mlp, w_up, w_down):
    off_kv, off_kr = Q_LORA, Q_LORA + KV_LORA
    off_p = off_kr + QK_ROPE
    wi = w_in[l]
    d = wi.shape[0]
    w1 = jnp.concatenate([wi[:, :off_p], jnp.zeros((d, LANE - QK_ROPE), F32)], axis=1)
    uq = w_uq[l]
    ukv = w_ukv[l]
    k_nope, v = ukv[..., :QK_NOPE], ukv[..., QK_NOPE:]
    zr = jnp.zeros(k_nope.shape[:-1] + (QK_ROPE,), F32)
    gq, gk = g_q[l], g_k[l]
    slab_gain = lambda g: jnp.concatenate([g[QK_NOPE:], g[:QK_NOPE], jnp.zeros((LANE - QK_HEAD,), F32)])[None]
    wc = jnp.zeros((N_HEADS, LANE, KV_LORA), F32).at[:, QK_ROPE:QK_HEAD, :].set(k_nope.transpose(1, 2, 0))
    ncol = N_HEADS * t_new
    e = (jnp.arange(N_HEADS * QK_NOPE)[:, None] // QK_NOPE == jnp.arange(ncol)[None, :] // t_new)
    return {
        "g_mix": g_mix[l][None],
        "w1": w1.astype(BF16),
        "w2": wi[:, off_p:].astype(BF16),
        "g_qa": g_qa[l][None],
        "g_kva": g_kva[l][None],
        "wq": _slab(uq[..., QK_NOPE:], uq[..., :QK_NOPE]).astype(BF16),
        "wq_t": _slab(uq[..., QK_NOPE:], uq[..., :QK_NOPE]).T.astype(BF16),
        "wk": _slab(zr, k_nope).astype(BF16),
        "wv": v.reshape(KV_LORA, ATTN_WIDTH).astype(BF16),
        "wv_t": v.reshape(KV_LORA, ATTN_WIDTH).T.astype(BF16),
        "wuk": k_nope.reshape(KV_LORA, N_HEADS * QK_NOPE).astype(BF16),
        "wc": wc.astype(BF16),
        "e": e.astype(BF16),
        "gq": slab_gain(gq) * (SM_SCALE * LOG2E),
        "gq_t": slab_gain(gq).T * (SM_SCALE * LOG2E),
        "gk": slab_gain(gk),
        "gk1": slab_gain(gk),
        "w_attn_out": w_attn_out[l].astype(BF16),
        "w_pool": w_pool[l].astype(BF16),
        "pool_scale": pool_scale[l][None],
        "w_pool_out": w_pool_out[l].astype(BF16),
        "w_o": w_o[l].astype(BF16),
        "g_mlp": g_mlp[l][None],
        "w_up": w_up[l].astype(BF16),
        "w_down": w_down[l].astype(BF16),
    }


def kernel(x_prompt, x_sample, cache_ckv, cache_krope, state_pool, g_mix, w_in, g_qa, g_kva, w_uq, w_ukv,
           g_q, g_k, w_attn_out, w_pool, pool_scale, w_pool_out, w_o, g_mlp, w_up, w_down):
    batch, seq, d = x_prompt.shape
    nseq, t_new, _ = x_sample.shape
    depth = g_mix.shape[0]
    past = cache_ckv.shape[2]

    tabs_p = _rope_tables(jnp.arange(seq, dtype=F32))
    proj_tm_s = min(PROJ_TM, nseq * t_new)
    reps = proj_tm_s // t_new
    tabs_s = _rope_tables(past + jnp.arange(t_new, dtype=F32))
    tabs_s = tuple(jnp.tile(a, (reps, 1)) for a in tabs_s[:3]) + tuple(jnp.tile(a, (1, reps)) for a in tabs_s[3:])
    hist = jnp.pad(state_pool, ((0, 0), (0, 0), (POOL_PAD - POOL_HIST, 0), (0, 0)))
    krope_t = jnp.swapaxes(cache_krope, 2, 3)

    xp = x_prompt.reshape(batch * seq, d)
    xs = x_sample.reshape(nseq * t_new, d)
    outs = {k: [] for k in ("kr_p", "pool_p", "kr_s", "pool_s")}
    ckv_p = ckv_s = None
    for l in range(depth):
        w = _layer_weights(l, t_new, g_mix, w_in, g_qa, g_kva, w_uq, w_ukv, g_q, g_k, w_attn_out, w_pool,
                           pool_scale, w_pool_out, w_o, g_mlp, w_up, w_down)
        qt, k, vt, ckv_p, kr = _proj(xp, tabs_p, seq // PROJ_TM, w, PROJ_TM, True, l, depth, ckv_p)
        attn = _flash(qt, k, vt, batch, seq)
        xp, pool = _merge(xp, attn, None, w, batch, seq, MERGE_TM, 0)
        outs["kr_p"].append(kr.reshape(batch, seq, QK_ROPE))
        outs["pool_p"].append(pool[:, POOL_PAD - POOL_HIST:])

        q, ckv_s, kr = _proj(xs, tabs_s, 1, w, proj_tm_s, False, l, depth, ckv_s)
        attn = _cache_attn(q, cache_ckv, krope_t, l, ckv_s[l], kr, w, nseq, t_new, CACHE_TK)
        xs, pool = _merge(xs, attn, hist[l], w, nseq, t_new, min(MERGE_TM, nseq * t_new), past)
        outs["kr_s"].append(kr.reshape(nseq, t_new, QK_ROPE))
        outs["pool_s"].append(pool[:, POOL_PAD - POOL_HIST:])

    return (xp.reshape(batch, seq, d), xs.reshape(nseq, t_new, d),
            ckv_p.reshape(depth, batch, seq, KV_LORA), jnp.stack(outs["kr_p"]), jnp.stack(outs["pool_p"]),
            ckv_s.reshape(depth, nseq, t_new, KV_LORA), jnp.stack(outs["kr_s"]), jnp.stack(outs["pool_s"]))
```

```python
import functools

import jax
import jax.numpy as jnp
from jax import lax
from jax.experimental import pallas as pl
from jax.experimental.pallas import tpu as pltpu

F32 = jnp.float32
BF16 = jnp.bfloat16

LANE = 128
CHUNK = 64
N_HEADS = 8
QK_NOPE = 64
QK_ROPE = 32
QK_HEAD = QK_NOPE + QK_ROPE
V_HEAD = 64
Q_LORA = 384
KV_LORA = 256
ATTN_WIDTH = N_HEADS * V_HEAD
VT_ROWS = V_HEAD + 16
POOL_WINDOWS = (2, 4, 8, 16)
POOL_GROUP = 128
POOL_WIDTH = len(POOL_WINDOWS) * POOL_GROUP
POOL_HIST = max(POOL_WINDOWS) - 1
POOL_PAD = POOL_HIST + 1
ROPE_THETA = 10000.0
EPS = 1e-6
SM_SCALE = QK_HEAD ** -0.5
LOG2E = 1.4426950408889634
SLAB = N_HEADS * LANE
NEG = -1e30

PROJ_TM = 1024
ATT_T = 256
MERGE_TM = 1024
CACHE_TK = 512
CACHE_GROUP = 4
VMEM_LIMIT = 62 * 1024 * 1024


def _const_spec(shape):
    nd = len(shape)
    return pl.BlockSpec(shape, lambda *_: (0,) * nd, pipeline_mode=pl.Buffered(1))


def _rms(x, g):
    ms = jnp.mean(x * x, axis=-1, keepdims=True)
    return x * lax.rsqrt(ms + EPS) * g


def _rope_slab(x, c, s1, s2):
    return x * c + pltpu.roll(x, 16, 1) * s1 + pltpu.roll(x, LANE - 16, 1) * s2


def _head_norm(x, g):
    ss = jnp.sum(x * x, axis=-1, keepdims=True)
    return x * lax.rsqrt(ss * (1.0 / QK_HEAD) + EPS) * g


def _proj_kernel(x_ref, c_ref, s1_ref, s2_ref, ct_ref, st_ref, gmix_ref, w1_ref, gqa_ref, gkva_ref, wq_ref,
                 wqt_ref, wk_ref, wvt_ref, gq_ref, gqt_ref, gk_ref, *out_refs, transposed, aliased):
    if aliased:
        out_refs = out_refs[1:]
    if transposed:
        q_ref, k_ref, v_ref, ckv_ref, kr_ref = out_refs
    else:
        q_ref, ckv_ref, kr_ref = out_refs
    half = QK_ROPE // 2
    for t in range(x_ref.shape[0] // ATT_T):
        rows = slice(t * ATT_T, (t + 1) * ATT_T)
        h = _rms(x_ref[rows, :], gmix_ref[...]).astype(BF16)
        z = jnp.dot(h, w1_ref[...], preferred_element_type=F32)
        cq = _rms(z[:, :Q_LORA], gqa_ref[...]).astype(BF16)
        ckv = _rms(z[:, Q_LORA:Q_LORA + KV_LORA], gkva_ref[...])
        ckv_ref[rows, :] = ckv
        c, s1, s2 = c_ref[rows, :], s1_ref[rows, :], s2_ref[rows, :]
        kr = _rope_slab(z[:, Q_LORA + KV_LORA:], c, s1, s2)
        kr_ref[rows, :] = kr[:, :QK_ROPE]
        if not transposed:
            q = jnp.dot(cq, wq_ref[...], preferred_element_type=F32)
            for hd in range(N_HEADS):
                sl = slice(hd * LANE, (hd + 1) * LANE)
                q_ref[rows, sl] = _head_norm(_rope_slab(q[:, sl], c, s1, s2), gq_ref[...]).astype(BF16)
            continue
        ckv_b = ckv.astype(BF16)
        kn = jnp.dot(ckv_b, wk_ref[...], preferred_element_type=F32)
        nt = (((1,), (1,)), ((), ()))
        vt = lax.dot_general(wvt_ref[...], ckv_b, nt, preferred_element_type=F32).astype(BF16)
        qt_all = lax.dot_general(wqt_ref[...], cq, nt, preferred_element_type=F32)
        ones = jnp.ones((VT_ROWS - V_HEAD, ATT_T), BF16)
        cos, sin = ct_ref[:, rows], st_ref[:, rows]
        for hd in range(N_HEADS):
            sl = slice(hd * LANE, (hd + 1) * LANE)
            v_ref[t, hd * VT_ROWS:hd * VT_ROWS + V_HEAD, :] = vt[hd * V_HEAD:(hd + 1) * V_HEAD, :]
            v_ref[t, hd * VT_ROWS + V_HEAD:(hd + 1) * VT_ROWS, :] = ones
            qt = qt_all[sl, :]
            x1, x2 = qt[:half], qt[half:QK_ROPE]
            qt = jnp.concatenate([x1 * cos - x2 * sin, x1 * sin + x2 * cos, qt[QK_ROPE:]], axis=0)
            ss = jnp.sum(qt * qt, axis=0, keepdims=True)
            q_ref[t, sl, :] = (qt * lax.rsqrt(ss * (1.0 / QK_HEAD) + EPS) * gqt_ref[...]).astype(BF16)
            k_ref[hd, rows, :] = _head_norm(kn[:, sl] + kr, gk_ref[...]).astype(BF16)


def _proj(x, tabs, tab_tiles, w, tm, transposed, layer, depth, ckv_all):
    t = x.shape[0]
    d = x.shape[1]
    assert t % tm == 0 and tm % ATT_T == 0
    c_tab, s1_tab, s2_tab, cos_t, sin_t = tabs
    row = lambda i: (i, 0)
    lead = lambda i: (i, 0, 0)
    tab = pl.BlockSpec((tm, LANE), lambda i: (i % tab_tiles, 0))
    tab_t = pl.BlockSpec((QK_ROPE // 2, tm), lambda i: (0, i % tab_tiles))
    weights = (w["g_mix"], w["w1"], w["g_qa"], w["g_kva"], w["wq"], w["wq_t"], w["wk"], w["wv_t"], w["gq"],
               w["gq_t"], w["gk"])
    in_specs = [pl.BlockSpec((tm, d), row), tab, tab, tab, tab_t, tab_t] + [
        _const_spec(a.shape) for a in weights]
    latent_shapes = (jax.ShapeDtypeStruct((depth, t, KV_LORA), F32), jax.ShapeDtypeStruct((t, QK_ROPE), F32))
    latent_specs = (pl.BlockSpec((None, tm, KV_LORA), lambda i: (layer, i, 0)), pl.BlockSpec((tm, QK_ROPE), row))
    if transposed:
        nt = tm // ATT_T
        out_shape = (
            jax.ShapeDtypeStruct((t // ATT_T, SLAB, ATT_T), BF16),
            jax.ShapeDtypeStruct((N_HEADS, t, LANE), BF16),
            jax.ShapeDtypeStruct((t // ATT_T, N_HEADS * VT_ROWS, ATT_T), BF16),
        ) + latent_shapes
        out_specs = (
            pl.BlockSpec((nt, SLAB, ATT_T), lead),
            pl.BlockSpec((N_HEADS, tm, LANE), lambda i: (0, i, 0)),
            pl.BlockSpec((nt, N_HEADS * VT_ROWS, ATT_T), lead),
        ) + latent_specs
    else:
        out_shape = (jax.ShapeDtypeStruct((t, SLAB), BF16),) + latent_shapes
        out_specs = (pl.BlockSpec((tm, SLAB), row),) + latent_specs
    args = [x, c_tab, s1_tab, s2_tab, cos_t, sin_t, *weights]
    aliases = {}
    if ckv_all is not None:
        in_specs.append(pl.BlockSpec(memory_space=pl.ANY))
        aliases = {len(args): len(out_shape) - 2}
        args.append(ckv_all)
    return pl.pallas_call(
        functools.partial(_proj_kernel, transposed=transposed, aliased=ckv_all is not None),
        grid=(t // tm,),
        in_specs=in_specs,
        out_specs=out_specs,
        out_shape=out_shape,
        input_output_aliases=aliases,
        compiler_params=pltpu.CompilerParams(dimension_semantics=("parallel",),
                                             vmem_limit_bytes=VMEM_LIMIT),
        name="proj",
    )(*args)


def _flash_kernel(qt_ref, k_ref, vt_ref, o_ref, m_ref, acc_ref, s_ref, mt_ref):
    t = ATT_T
    g = pl.program_id(1)

    def score_head(slot, tile, masked, hd, qx):
        start = pl.multiple_of(tile * t, t)
        qsl = slice(hd * LANE, (hd + 1) * LANE)
        s = jnp.dot(k_ref[hd, pl.ds(start, t), :], qt_ref[qx, qsl, :], preferred_element_type=F32)
        if masked:
            k_chunk = lax.broadcasted_iota(jnp.int32, (t, t), 0) // CHUNK
            q_chunk = lax.broadcasted_iota(jnp.int32, (t, t), 1) // CHUNK
            s = jnp.where(k_chunk <= q_chunk, s, NEG)
        s_ref[slot, hd] = s
        mt_ref[slot, hd:hd + 1, :] = jnp.max(s, axis=0, keepdims=True)

    def value_head(slot, tile, hd, qx):
        vsl = slice(hd * VT_ROWS, (hd + 1) * VT_ROWS)
        m_old = m_ref[qx, hd:hd + 1, :]
        m_new = jnp.maximum(m_old, mt_ref[slot, hd:hd + 1, :])
        alpha = jnp.exp2(m_old - m_new)
        p = jnp.exp2(s_ref[slot, hd] - m_new).astype(BF16)
        pv = jnp.dot(vt_ref[tile, vsl, :], p, preferred_element_type=F32)
        acc_ref[qx, vsl, :] = alpha * acc_ref[qx, vsl, :] + pv
        m_ref[qx, hd:hd + 1, :] = m_new

    def value_phase(slot, tile, qx):
        for hd in range(N_HEADS):
            value_head(slot, tile, hd, qx)

    def both_phases(score_slot, score_tile, score_q, value_slot, value_tile, value_q, masked=False):
        for hd in range(N_HEADS):
            score_head(score_slot, score_tile, masked, hd, score_q)
            value_head(value_slot, value_tile, hd, value_q)

    def finalize(qx):
        outs = []
        for hd in range(N_HEADS):
            base = hd * VT_ROWS
            outs.append(acc_ref[qx, base:base + V_HEAD, :] / acc_ref[qx, base + V_HEAD:base + V_HEAD + 1, :])
        o_ref[qx * t:(qx + 1) * t, :] = jnp.concatenate(outs, axis=0).T.astype(BF16)

    def attend(qx, chained):
        i = 2 * g + qx
        s0, s1 = qx, 1 - qx
        m_ref[qx] = jnp.full(m_ref.shape[1:], NEG, F32)
        acc_ref[qx] = jnp.zeros(acc_ref.shape[1:], F32)
        if not chained:
            for hd in range(N_HEADS):
                score_head(s0, i, True, hd, qx)

        def tile_pair(first, pending):
            both_phases(s1, first, qx, s0, pending, qx)
            both_phases(s0, first + 1, qx, s1, first, qx)

        n_quads = i // 4

        def quad(p, carry):
            tile_pair(4 * p, jnp.where(p == 0, i, 4 * p - 1))
            tile_pair(4 * p + 2, 4 * p + 1)
            return carry

        lax.fori_loop(0, n_quads, quad, 0)
        done = 4 * n_quads
        pending = jnp.where(n_quads == 0, i, done - 1)
        has_pair = i - done >= 2

        @pl.when(has_pair)
        def _():
            tile_pair(done, pending)

        done = jnp.where(has_pair, done + 2, done)
        pending = jnp.where(has_pair, done - 1, pending)
        if qx == 0:
            both_phases(s1, i + 1, 1, s0, pending, qx, masked=True)
        else:
            if chained:
                finalize(qx - 1)
            both_phases(s1, done, qx, s0, pending, qx)
            value_phase(s1, done, qx)

    attend(0, False)
    attend(1, True)
    finalize(1)


def _flash(qt, k, vt, batch, seq):
    assert seq % (2 * ATT_T) == 0 and ATT_T % CHUNK == 0
    nq = seq // ATT_T
    steps = nq // 2
    return pl.pallas_call(
        _flash_kernel,
        grid=(batch, steps),
        in_specs=[
            pl.BlockSpec((2, SLAB, ATT_T), lambda b, g: (b * steps + g, 0, 0)),
            pl.BlockSpec((N_HEADS, seq, LANE), lambda b, g: (0, b, 0)),
            pl.BlockSpec((nq, N_HEADS * VT_ROWS, ATT_T), lambda b, g: (b, 0, 0)),
        ],
        out_specs=pl.BlockSpec((2 * ATT_T, ATTN_WIDTH), lambda b, g: (b * steps + g, 0)),
        out_shape=jax.ShapeDtypeStruct((batch * seq, ATTN_WIDTH), BF16),
        scratch_shapes=[pltpu.VMEM((2, N_HEADS, ATT_T), F32), pltpu.VMEM((2, N_HEADS * VT_ROWS, ATT_T), F32),
                        pltpu.VMEM((2, N_HEADS, ATT_T, ATT_T), F32), pltpu.VMEM((2, N_HEADS, ATT_T), F32)],
        compiler_params=pltpu.CompilerParams(dimension_semantics=("parallel", "parallel"),
                                             vmem_limit_bytes=VMEM_LIMIT),
        name="flash",
    )(qt, k, vt)


def _cache_attn_kernel(q_ref, cache_ref, ckr_ref, cnew_ref, krnew_ref, gk_ref, wc_ref, wuk_ref, e_ref,
                       wuv_ref, o_ref, s_ref, *, past, t_new, tk):
    n_tiles = past // tk
    ncol = N_HEADS * t_new
    gk = gk_ref[...]
    qt, qr = [], []
    for hd in range(N_HEADS):
        qg = q_ref[:, hd * LANE:(hd + 1) * LANE].astype(F32) * gk
        qt.append(jnp.dot(qg.astype(BF16), wc_ref[hd], preferred_element_type=F32))
        qr.append(qg[:, :QK_ROPE])
    qt = jnp.concatenate(qt, axis=0).astype(BF16)
    qr = jnp.concatenate(qr, axis=0).astype(BF16)

    nt = (((1,), (1,)), ((), ()))

    def scores_many(tiles):
        cbs = [c.astype(BF16) for c, _ in tiles]
        kns = [jnp.dot(cb, wuk_ref[...], preferred_element_type=F32) for cb in cbs]
        ssqs = [jnp.dot((kn * kn).astype(BF16), e_ref[...], preferred_element_type=F32) for kn in kns]
        invs = [lax.rsqrt((ssq + jnp.sum(kr * kr, axis=-1, keepdims=True)) * (1.0 / QK_HEAD) + EPS)
                for ssq, (_, kr) in zip(ssqs, tiles)]
        ss = [lax.dot_general(cb, qt, nt, preferred_element_type=F32) for cb in cbs]
        ss = [s + lax.dot_general(kr.astype(BF16), qr, nt, preferred_element_type=F32)
              for s, (_, kr) in zip(ss, tiles)]
        return [s * inv for s, inv in zip(ss, invs)]

    def scores(c, kr):
        return scores_many([(c, kr)])[0]

    m = jnp.full((1, ncol), NEG, F32)
    for first in range(0, n_tiles, CACHE_GROUP):
        starts = [j * tk for j in range(first, min(first + CACHE_GROUP, n_tiles))]
        tiles = [(cache_ref[pl.ds(st, tk), :], ckr_ref[:, pl.ds(st, tk)].T) for st in starts]
        for st, s in zip(starts, scores_many(tiles)):
            s_ref[pl.ds(st, tk), :] = s
            m = jnp.maximum(m, jnp.max(s, axis=0, keepdims=True))
    c_new = cnew_ref[...]
    s_new = scores(c_new, krnew_ref[...])
    m = jnp.maximum(m, jnp.max(s_new, axis=0, keepdims=True))

    tn = (((0,), (0,)), ((), ()))

    def accumulate(s, c):
        p = jnp.exp2(s - m)
        ctx = lax.dot_general(p.astype(BF16), c.astype(BF16), tn, preferred_element_type=F32)
        return jnp.sum(p, axis=0, keepdims=True), ctx

    l_old, ctx_old = accumulate(s_ref[...], cache_ref[...])
    l_new, ctx_new = accumulate(s_new, c_new)
    l, ctx = l_old + l_new, ctx_old + ctx_new

    eye = lax.broadcasted_iota(jnp.int32, (ncol, ncol), 0) == lax.broadcasted_iota(jnp.int32, (ncol, ncol), 1)
    l_col = jnp.sum(jnp.where(eye, jnp.broadcast_to(l, (ncol, ncol)), 0.0), axis=1, keepdims=True)
    ctx = (ctx / l_col).astype(BF16)
    r = jnp.dot(ctx, wuv_ref[...], preferred_element_type=F32)
    col_head = lax.broadcasted_iota(jnp.int32, (t_new, ATTN_WIDTH), 1) // V_HEAD
    out = jnp.zeros((t_new, ATTN_WIDTH), F32)
    for hd in range(N_HEADS):
        out = out + jnp.where(col_head == hd, r[hd * t_new:(hd + 1) * t_new, :], 0.0)
    o_ref[...] = out.astype(BF16)


def _cache_attn(q, cache_ckv, cache_krope_t, layer, c_new, kr_new, w, nseq, t_new, tk):
    past = cache_ckv.shape[2]
    assert past % tk == 0
    ncol = N_HEADS * t_new
    return pl.pallas_call(
        functools.partial(_cache_attn_kernel, past=past, t_new=t_new, tk=tk),
        grid=(nseq,),
        in_specs=[
            pl.BlockSpec((t_new, SLAB), lambda b: (b, 0)),
            pl.BlockSpec((None, None, past, KV_LORA), lambda b: (layer, b, 0, 0)),
            pl.BlockSpec((None, None, QK_ROPE, past), lambda b: (layer, b, 0, 0)),
            pl.BlockSpec((t_new, KV_LORA), lambda b: (b, 0)),
            pl.BlockSpec((t_new, QK_ROPE), lambda b: (b, 0)),
        ] + [_const_spec(a.shape) for a in (w["gk1"], w["wc"], w["wuk"], w["e"], w["wv"])],
        out_specs=pl.BlockSpec((t_new, ATTN_WIDTH), lambda b: (b, 0)),
        out_shape=jax.ShapeDtypeStruct((nseq * t_new, ATTN_WIDTH), BF16),
        scratch_shapes=[pltpu.VMEM((past, ncol), F32)],
        compiler_params=pltpu.CompilerParams(dimension_semantics=("parallel",),
                                             vmem_limit_bytes=VMEM_LIMIT),
        name="cache_attn",
    )(q, cache_ckv, cache_krope_t, c_new, kr_new, w["gk1"], w["wc"], w["wuk"], w["e"], w["wv"])


def _merge_kernel(*refs, groups, rows, tiles_per_seq, pos0, has_hist):
    if has_hist:
        x_ref, a_ref, hist_ref = refs[:3]
        refs = refs[3:]
    else:
        x_ref, a_ref = refs[:2]
        hist_ref = None
        refs = refs[2:]
    (gmix_ref, w2_ref, wpool_ref, pscale_ref, wao_ref, wpo_ref, wo_ref, gmlp_ref, wup_ref, wdown_ref,
     y_ref, pool_ref, ext_ref) = refs
    tm = groups * rows
    tile = pl.program_id(0) % tiles_per_seq

    x = x_ref[...]
    h = _rms(x, gmix_ref[...]).astype(BF16)
    z = jnp.dot(h, w2_ref[...], preferred_element_type=F32)
    d = x.shape[1]
    p = z[:, :POOL_WIDTH]
    p3 = p.reshape(groups, rows, POOL_WIDTH)

    group_slices = [slice(g * POOL_GROUP, (g + 1) * POOL_GROUP) for g in range(len(POOL_WINDOWS))]
    for g, sl in enumerate(group_slices):
        if has_hist:
            ext_ref[g, :, :POOL_PAD, :] = hist_ref[:, :, sl]
        else:
            @pl.when(tile == 0)
            def _():
                ext_ref[g, :, :POOL_PAD, :] = jnp.zeros((groups, POOL_PAD, POOL_GROUP), F32)

            @pl.when(tile != 0)
            def _():
                ext_ref[g, :, :POOL_PAD, :] = ext_ref[g, :, rows:rows + POOL_PAD, :]
        ext_ref[g, :, POOL_PAD:, :] = p3[:, :, sl]
    pool_ref[...] = p3[:, rows - POOL_PAD:, :]

    pos = pos0 + tile * rows + lax.broadcasted_iota(jnp.int32, (groups, rows, POOL_GROUP), 1)
    us = []
    for g, win in enumerate(POOL_WINDOWS):
        sl = slice(g * POOL_GROUP, (g + 1) * POOL_GROUP)
        tot = p3[:, :, sl]
        for j in range(1, win):
            tot = tot + ext_ref[g, :, pl.ds(POOL_PAD - j, rows), :]
        cnt = jnp.minimum(win, pos + 1).astype(F32)
        pooled = (tot / cnt - p3[:, :, sl]).reshape(tm, POOL_GROUP)
        u = jnp.dot(pooled.astype(BF16), wpool_ref[g], preferred_element_type=F32)
        us.append((u * pscale_ref[:, sl]).astype(BF16))
    u = jnp.concatenate(us, axis=1)

    branch_a = jnp.dot(a_ref[...], wao_ref[...], preferred_element_type=F32)
    branch_b = jnp.dot(u, wpo_ref[...], preferred_element_type=F32)
    gate_a = 1.0 / (1.0 + jnp.exp(-z[:, POOL_WIDTH:POOL_WIDTH + d]))
    gate_b = 1.0 / (1.0 + jnp.exp(-z[:, POOL_WIDTH + d:]))
    mix = (gate_a * branch_a + gate_b * branch_b).astype(BF16)
    x = x + jnp.dot(mix, wo_ref[...], preferred_element_type=F32)
    hm = _rms(x, gmlp_ref[...]).astype(BF16)
    up = jnp.dot(hm, wup_ref[...], preferred_element_type=F32)
    act = jnp.square(jnp.maximum(up, 0.0)).astype(BF16)
    y_ref[...] = x + jnp.dot(act, wdown_ref[...], preferred_element_type=F32)


def _merge(x, attn, hist, w, nseq, seq, tm, pos0):
    t, d = x.shape
    assert t % tm == 0
    if hist is None:
        assert seq % tm == 0
        groups, rows, tiles_per_seq = 1, tm, seq // tm
    else:
        assert tm % seq == 0 and seq >= POOL_PAD
        groups, rows, tiles_per_seq = tm // seq, seq, 1
    assert rows % 8 == 0 and rows >= POOL_PAD
    row = lambda i: (i, 0)
    seq_block = lambda i: (i // tiles_per_seq, 0, 0)
    in_specs = [pl.BlockSpec((tm, d), row), pl.BlockSpec((tm, ATTN_WIDTH), row)]
    args = [x, attn]
    if hist is not None:
        in_specs.append(pl.BlockSpec((groups, POOL_PAD, POOL_WIDTH), seq_block))
        args.append(hist)
    weights = (w["g_mix"], w["w2"], w["w_pool"], w["pool_scale"], w["w_attn_out"], w["w_pool_out"],
               w["w_o"], w["g_mlp"], w["w_up"], w["w_down"])
    in_specs += [_const_spec(a.shape) for a in weights]
    return pl.pallas_call(
        functools.partial(_merge_kernel, groups=groups, rows=rows, tiles_per_seq=tiles_per_seq,
                          pos0=pos0, has_hist=hist is not None),
        grid=(t // tm,),
        in_specs=in_specs,
        out_specs=(pl.BlockSpec((tm, d), row),
                   pl.BlockSpec((groups, POOL_PAD, POOL_WIDTH), seq_block)),
        out_shape=(jax.ShapeDtypeStruct((t, d), F32),
                   jax.ShapeDtypeStruct((nseq, POOL_PAD, POOL_WIDTH), F32)),
        scratch_shapes=[pltpu.VMEM((len(POOL_WINDOWS), groups, POOL_PAD + rows, POOL_GROUP), F32)],
        compiler_params=pltpu.CompilerParams(dimension_semantics=("arbitrary",),
                                             vmem_limit_bytes=VMEM_LIMIT),
        name="merge",
    )(*args, *weights)


def _rope_tables(pos):
    half = QK_ROPE // 2
    inv = jnp.power(ROPE_THETA, -jnp.arange(half, dtype=F32) / half)
    ang = pos[:, None] * inv[None, :]
    cos, sin = jnp.cos(ang), jnp.sin(ang)
    n = pos.shape[0]
    one = jnp.ones((n, LANE - QK_ROPE), F32)
    zero = jnp.zeros((n, LANE - half), F32)
    c = jnp.concatenate([cos, cos, one], axis=1)
    s1 = jnp.concatenate([jnp.zeros((n, half), F32), sin, jnp.zeros((n, LANE - QK_ROPE), F32)], axis=1)
    s2 = jnp.concatenate([-sin, zero], axis=1)
    return c, s1, s2, cos.T, sin.T


def _slab(rope_part, nope_part):
    pad = jnp.zeros(rope_part.shape[:-1] + (LANE - QK_HEAD,), rope_part.dtype)
    s = jnp.concatenate([rope_part, nope_part, pad], axis=-1)
    return s.reshape(s.shape[:-2] + (N_HEADS * LANE,))


def _layer_weights(l, t_new, g_mix, w_in, g_qa, g_kva, w_uq, w_ukv, g_q, g_k, w_attn_out, w_pool,
                   pool_scale, w_pool_out, w_o, g_mlp, w_up, w_down):
    off_kv, off_kr = Q_LORA, Q_LORA + KV_LORA
    off_p = off_kr + QK_ROPE
    wi = w_in[l]
    d = wi.shape[0]
    w1 = jnp.concatenate([wi[:, :off_p], jnp.zeros((d, LANE - QK_ROPE), F32)], axis=1)
    uq = w_uq[l]
    ukv = w_ukv[l]
    k_nope, v = ukv[..., :QK_NOPE], ukv[..., QK_NOPE:]
    zr = jnp.zeros(k_nope.shape[:-1] + (QK_ROPE,), F32)
    gq, gk = g_q[l], g_k[l]
    slab_gain = lambda g: jnp.concatenate([g[QK_NOPE:], g[:QK_NOPE], jnp.zeros((LANE - QK_HEAD,), F32)])[None]
    wc = jnp.zeros((N_HEADS, LANE, KV_LORA), F32).at[:, QK_ROPE:QK_HEAD, :].set(k_nope.transpose(1, 2, 0))
    ncol = N_HEADS * t_new
    e = (jnp.arange(N_HEADS * QK_NOPE)[:, None] // QK_NOPE == jnp.arange(ncol)[None, :] // t_new)
    return {
        "g_mix": g_mix[l][None],
        "w1": w1.astype(BF16),
        "w2": wi[:, off_p:].astype(BF16),
        "g_qa": g_qa[l][None],
        "g_kva": g_kva[l][None],
        "wq": _slab(uq[..., QK_NOPE:], uq[..., :QK_NOPE]).astype(BF16),
        "wq_t": _slab(uq[..., QK_NOPE:], uq[..., :QK_NOPE]).T.astype(BF16),
        "wk": _slab(zr, k_nope).astype(BF16),
        "wv": v.reshape(KV_LORA, ATTN_WIDTH).astype(BF16),
        "wv_t": v.reshape(KV_LORA, ATTN_WIDTH).T.astype(BF16),
        "wuk": k_nope.reshape(KV_LORA, N_HEADS * QK_NOPE).astype(BF16),
        "wc": wc.astype(BF16),
        "e": e.astype(BF16),
        "gq": slab_gain(gq) * (SM_SCALE * LOG2E),
        "gq_t": slab_gain(gq).T * (SM_SCALE * LOG2E),
        "gk": slab_gain(gk),
        "gk1": slab_gain(gk),
        "w_attn_out": w_attn_out[l].astype(BF16),
        "w_pool": w_pool[l].astype(BF16),
        "pool_scale": pool_scale[l][None],
        "w_pool_out": w_pool_out[l].astype(BF16),
        "w_o": w_o[l].astype(BF16),
        "g_mlp": g_mlp[l][None],
        "w_up": w_up[l].astype(BF16),
        "w_down": w_down[l].astype(BF16),
    }


def kernel(x_prompt, x_sample, cache_ckv, cache_krope, state_pool, g_mix, w_in, g_qa, g_kva, w_uq, w_ukv,
           g_q, g_k, w_attn_out, w_pool, pool_scale, w_pool_out, w_o, g_mlp, w_up, w_down):
    batch, seq, d = x_prompt.shape
    nseq, t_new, _ = x_sample.shape
    depth = g_mix.shape[0]
    past = cache_ckv.shape[2]

    tabs_p = _rope_tables(jnp.arange(seq, dtype=F32))
    proj_tm_s = min(PROJ_TM, nseq * t_new)
    reps = proj_tm_s // t_new
    tabs_s = _rope_tables(past + jnp.arange(t_new, dtype=F32))
    tabs_s = tuple(jnp.tile(a, (reps, 1)) for a in tabs_s[:3]) + tuple(jnp.tile(a, (1, reps)) for a in tabs_s[3:])
    hist = jnp.pad(state_pool, ((0, 0), (0, 0), (POOL_PAD - POOL_HIST, 0), (0, 0)))
    krope_t = jnp.swapaxes(cache_krope, 2, 3)

    xp = x_prompt.reshape(batch * seq, d)
    xs = x_sample.reshape(nseq * t_new, d)
    outs = {k: [] for k in ("kr_p", "pool_p", "kr_s", "pool_s")}
    ckv_p = ckv_s = None
    for l in range(depth):
        w = _layer_weights(l, t_new, g_mix, w_in, g_qa, g_kva, w_uq, w_ukv, g_q, g_k, w_attn_out, w_pool,
                           pool_scale, w_pool_out, w_o, g_mlp, w_up, w_down)
        qt, k, vt, ckv_p, kr = _proj(xp, tabs_p, seq // PROJ_TM, w, PROJ_TM, True, l, depth, ckv_p)
        attn = _flash(qt, k, vt, batch, seq)
        xp, pool = _merge(xp, attn, None, w, batch, seq, MERGE_TM, 0)
        outs["kr_p"].append(kr.reshape(batch, seq, QK_ROPE))
        outs["pool_p"].append(pool[:, POOL_PAD - POOL_HIST:])

        q, ckv_s, kr = _proj(xs, tabs_s, 1, w, proj_tm_s, False, l, depth, ckv_s)
        attn = _cache_attn(q, cache_ckv, krope_t, l, ckv_s[l], kr, w, nseq, t_new, CACHE_TK)
        xs, pool = _merge(xs, attn, hist[l], w, nseq, t_new, min(MERGE_TM, nseq * t_new), past)
        outs["kr_s"].append(kr.reshape(nseq, t_new, QK_ROPE))
        outs["pool_s"].append(pool[:, POOL_PAD - POOL_HIST:])

    return (xp.reshape(batch, seq, d), xs.reshape(nseq, t_new, d),
            ckv_p.reshape(depth, batch, seq, KV_LORA), jnp.stack(outs["kr_p"]), jnp.stack(outs["pool_p"]),
            ckv_s.reshape(depth, nseq, t_new, KV_LORA), jnp.stack(outs["kr_s"]), jnp.stack(outs["pool_s"]))
```

```python
import functools

import jax
import jax.numpy as jnp
from jax import lax
from jax.experimental import pallas as pl
from jax.experimental.pallas import tpu as pltpu

F32 = jnp.float32
BF16 = jnp.bfloat16

LANE = 128
CHUNK = 64
N_HEADS = 8
QK_NOPE = 64
QK_ROPE = 32
QK_HEAD = QK_NOPE + QK_ROPE
V_HEAD = 64
Q_LORA = 384
KV_LORA = 256
ATTN_WIDTH = N_HEADS * V_HEAD
VT_ROWS = V_HEAD + 16
POOL_WINDOWS = (2, 4, 8, 16)
POOL_GROUP = 128
POOL_WIDTH = len(POOL_WINDOWS) * POOL_GROUP
POOL_HIST = max(POOL_WINDOWS) - 1
POOL_PAD = POOL_HIST + 1
ROPE_THETA = 10000.0
EPS = 1e-6
SM_SCALE = QK_HEAD ** -0.5
LOG2E = 1.4426950408889634
SLAB = N_HEADS * LANE
NEG = -1e30

PROJ_TM = 1024
ATT_T = 256
MERGE_TM = 1024
CACHE_TK = 512
CACHE_GROUP = 4
VMEM_LIMIT = 62 * 1024 * 1024


def _const_spec(shape):
    nd = len(shape)
    return pl.BlockSpec(shape, lambda *_: (0,) * nd, pipeline_mode=pl.Buffered(1))


def _rms(x, g):
    ms = jnp.mean(x * x, axis=-1, keepdims=True)
    return x * lax.rsqrt(ms + EPS) * g


def _rope_slab(x, c, s1, s2):
    return x * c + pltpu.roll(x, 16, 1) * s1 + pltpu.roll(x, LANE - 16, 1) * s2


def _head_norm(x, g):
    ss = jnp.sum(x * x, axis=-1, keepdims=True)
    return x * lax.rsqrt(ss * (1.0 / QK_HEAD) + EPS) * g


def _proj_kernel(x_ref, c_ref, s1_ref, s2_ref, ct_ref, st_ref, gmix_ref, w1_ref, gqa_ref, gkva_ref, wq_ref,
                 wqt_ref, wk_ref, wvt_ref, gq_ref, gqt_ref, gk_ref, *out_refs, transposed, aliased):
    if aliased:
        out_refs = out_refs[1:]
    if transposed:
        q_ref, k_ref, v_ref, ckv_ref, kr_ref = out_refs
    else:
        q_ref, ckv_ref, kr_ref = out_refs
    half = QK_ROPE // 2
    for t in range(x_ref.shape[0] // ATT_T):
        rows = slice(t * ATT_T, (t + 1) * ATT_T)
        h = _rms(x_ref[rows, :], gmix_ref[...]).astype(BF16)
        z = jnp.dot(h, w1_ref[...], preferred_element_type=F32)
        cq = _rms(z[:, :Q_LORA], gqa_ref[...]).astype(BF16)
        ckv = _rms(z[:, Q_LORA:Q_LORA + KV_LORA], gkva_ref[...])
        ckv_ref[rows, :] = ckv
        c, s1, s2 = c_ref[rows, :], s1_ref[rows, :], s2_ref[rows, :]
        kr = _rope_slab(z[:, Q_LORA + KV_LORA:], c, s1, s2)
        kr_ref[rows, :] = kr[:, :QK_ROPE]
        if not transposed:
            q = jnp.dot(cq, wq_ref[...], preferred_element_type=F32)
            for hd in range(N_HEADS):
                sl = slice(hd * LANE, (hd + 1) * LANE)
                q_ref[rows, sl] = _head_norm(_rope_slab(q[:, sl], c, s1, s2), gq_ref[...]).astype(BF16)
            continue
        ckv_b = ckv.astype(BF16)
        kn = jnp.dot(ckv_b, wk_ref[...], preferred_element_type=F32)
        nt = (((1,), (1,)), ((), ()))
        vt = lax.dot_general(wvt_ref[...], ckv_b, nt, preferred_element_type=F32).astype(BF16)
        qt_all = lax.dot_general(wqt_ref[...], cq, nt, preferred_element_type=F32)
        ones = jnp.ones((VT_ROWS - V_HEAD, ATT_T), BF16)
        cos, sin = ct_ref[:, rows], st_ref[:, rows]
        for hd in range(N_HEADS):
            sl = slice(hd * LANE, (hd + 1) * LANE)
            v_ref[t, hd * VT_ROWS:hd * VT_ROWS + V_HEAD, :] = vt[hd * V_HEAD:(hd + 1) * V_HEAD, :]
            v_ref[t, hd * VT_ROWS + V_HEAD:(hd + 1) * VT_ROWS, :] = ones
            qt = qt_all[sl, :]
            x1, x2 = qt[:half], qt[half:QK_ROPE]
            qt = jnp.concatenate([x1 * cos - x2 * sin, x1 * sin + x2 * cos, qt[QK_ROPE:]], axis=0)
            ss = jnp.sum(qt * qt, axis=0, keepdims=True)
            q_ref[t, sl, :] = (qt * lax.rsqrt(ss * (1.0 / QK_HEAD) + EPS) * gqt_ref[...]).astype(BF16)
            k_ref[hd, rows, :] = _head_norm(kn[:, sl] + kr, gk_ref[...]).astype(BF16)


def _proj(x, tabs, tab_tiles, w, tm, transposed, layer, depth, ckv_all):
    t = x.shape[0]
    d = x.shape[1]
    assert t % tm == 0 and tm % ATT_T == 0
    c_tab, s1_tab, s2_tab, cos_t, sin_t = tabs
    row = lambda i: (i, 0)
    lead = lambda i: (i, 0, 0)
    tab = pl.BlockSpec((tm, LANE), lambda i: (i % tab_tiles, 0))
    tab_t = pl.BlockSpec((QK_ROPE // 2, tm), lambda i: (0, i % tab_tiles))
    weights = (w["g_mix"], w["w1"], w["g_qa"], w["g_kva"], w["wq"], w["wq_t"], w["wk"], w["wv_t"], w["gq"],
               w["gq_t"], w["gk"])
    in_specs = [pl.BlockSpec((tm, d), row), tab, tab, tab, tab_t, tab_t] + [
        _const_spec(a.shape) for a in weights]
    latent_shapes = (jax.ShapeDtypeStruct((depth, t, KV_LORA), F32), jax.ShapeDtypeStruct((t, QK_ROPE), F32))
    latent_specs = (pl.BlockSpec((None, tm, KV_LORA), lambda i: (layer, i, 0)), pl.BlockSpec((tm, QK_ROPE), row))
    if transposed:
        nt = tm // ATT_T
        out_shape = (
            jax.ShapeDtypeStruct((t // ATT_T, SLAB, ATT_T), BF16),
            jax.ShapeDtypeStruct((N_HEADS, t, LANE), BF16),
            jax.ShapeDtypeStruct((t // ATT_T, N_HEADS * VT_ROWS, ATT_T), BF16),
        ) + latent_shapes
        out_specs = (
            pl.BlockSpec((nt, SLAB, ATT_T), lead),
            pl.BlockSpec((N_HEADS, tm, LANE), lambda i: (0, i, 0)),
            pl.BlockSpec((nt, N_HEADS * VT_ROWS, ATT_T), lead),
        ) + latent_specs
    else:
        out_shape = (jax.ShapeDtypeStruct((t, SLAB), BF16),) + latent_shapes
        out_specs = (pl.BlockSpec((tm, SLAB), row),) + latent_specs
    args = [x, c_tab, s1_tab, s2_tab, cos_t, sin_t, *weights]
    aliases = {}
    if ckv_all is not None:
        in_specs.append(pl.BlockSpec(memory_space=pl.ANY))
        aliases = {len(args): len(out_shape) - 2}
        args.append(ckv_all)
    return pl.pallas_call(
        functools.partial(_proj_kernel, transposed=transposed, aliased=ckv_all is not None),
        grid=(t // tm,),
        in_specs=in_specs,
        out_specs=out_specs,
        out_shape=out_shape,
        input_output_aliases=aliases,
        compiler_params=pltpu.CompilerParams(dimension_semantics=("parallel",),
                                             vmem_limit_bytes=VMEM_LIMIT),
        name="proj",
    )(*args)


def _flash_kernel(qt_ref, k_ref, vt_ref, o_ref, m_ref, acc_ref, s_ref, mt_ref):
    t = ATT_T
    g = pl.program_id(1)

    def score_head(slot, tile, masked, hd, qx):
        start = pl.multiple_of(tile * t, t)
        qsl = slice(hd * LANE, (hd + 1) * LANE)
        s = jnp.dot(k_ref[hd, pl.ds(start, t), :], qt_ref[qx, qsl, :], preferred_element_type=F32)
        if masked:
            k_chunk = lax.broadcasted_iota(jnp.int32, (t, t), 0) // CHUNK
            q_chunk = lax.broadcasted_iota(jnp.int32, (t, t), 1) // CHUNK
            s = jnp.where(k_chunk <= q_chunk, s, NEG)
        s_ref[slot, hd] = s
        mt_ref[slot, hd:hd + 1, :] = jnp.max(s, axis=0, keepdims=True)

    def value_head(slot, tile, hd, qx):
        vsl = slice(hd * VT_ROWS, (hd + 1) * VT_ROWS)
        m_old = m_ref[qx, hd:hd + 1, :]
        m_new = jnp.maximum(m_old, mt_ref[slot, hd:hd + 1, :])
        alpha = jnp.exp2(m_old - m_new)
        p = jnp.exp2(s_ref[slot, hd] - m_new).astype(BF16)
        pv = jnp.dot(vt_ref[tile, vsl, :], p, preferred_element_type=F32)
        acc_ref[qx, vsl, :] = alpha * acc_ref[qx, vsl, :] + pv
        m_ref[qx, hd:hd + 1, :] = m_new

    def value_phase(slot, tile, qx):
        for hd in range(N_HEADS):
            value_head(slot, tile, hd, qx)

    def both_phases(score_slot, score_tile, score_q, value_slot, value_tile, value_q, masked=False):
        for hd in range(N_HEADS):
            score_head(score_slot, score_tile, masked, hd, score_q)
            value_head(value_slot, value_tile, hd, value_q)

    def finalize(qx):
        outs = []
        for hd in range(N_HEADS):
            base = hd * VT_ROWS
            outs.append(acc_ref[qx, base:base + V_HEAD, :] / acc_ref[qx, base + V_HEAD:base + V_HEAD + 1, :])
        o_ref[qx * t:(qx + 1) * t, :] = jnp.concatenate(outs, axis=0).T.astype(BF16)

    def attend(qx, chained):
        i = 2 * g + qx
        s0, s1 = qx, 1 - qx
        m_ref[qx] = jnp.full(m_ref.shape[1:], NEG, F32)
        acc_ref[qx] = jnp.zeros(acc_ref.shape[1:], F32)
        if not chained:
            for hd in range(N_HEADS):
                score_head(s0, i, True, hd, qx)

        def tile_pair(first, pending):
            both_phases(s1, first, qx, s0, pending, qx)
            both_phases(s0, first + 1, qx, s1, first, qx)

        n_octets = i // 8

        def octet(p, carry):
            tile_pair(8 * p, jnp.where(p == 0, i, 8 * p - 1))
            for first in (2, 4, 6):
                tile_pair(8 * p + first, 8 * p + first - 1)
            return carry

        lax.fori_loop(0, n_octets, octet, 0)
        done = 8 * n_octets
        pending = jnp.where(n_octets == 0, i, done - 1)
        has_quad = i - done >= 4

        @pl.when(has_quad)
        def _():
            tile_pair(done, pending)
            tile_pair(done + 2, done + 1)

        done = jnp.where(has_quad, done + 4, done)
        pending = jnp.where(has_quad, done - 1, pending)
        has_pair = i - done >= 2

        @pl.when(has_pair)
        def _():
            tile_pair(done, pending)

        done = jnp.where(has_pair, done + 2, done)
        pending = jnp.where(has_pair, done - 1, pending)
        if qx == 0:
            both_phases(s1, i + 1, 1, s0, pending, qx, masked=True)
        else:
            if chained:
                finalize(qx - 1)
            both_phases(s1, done, qx, s0, pending, qx)
            value_phase(s1, done, qx)

    attend(0, False)
    attend(1, True)
    finalize(1)


def _flash(qt, k, vt, batch, seq):
    assert seq % (2 * ATT_T) == 0 and ATT_T % CHUNK == 0
    nq = seq // ATT_T
    steps = nq // 2
    return pl.pallas_call(
        _flash_kernel,
        grid=(batch, steps),
        in_specs=[
            pl.BlockSpec((2, SLAB, ATT_T), lambda b, g: (b * steps + g, 0, 0)),
            pl.BlockSpec((N_HEADS, seq, LANE), lambda b, g: (0, b, 0)),
            pl.BlockSpec((nq, N_HEADS * VT_ROWS, ATT_T), lambda b, g: (b, 0, 0)),
        ],
        out_specs=pl.BlockSpec((2 * ATT_T, ATTN_WIDTH), lambda b, g: (b * steps + g, 0)),
        out_shape=jax.ShapeDtypeStruct((batch * seq, ATTN_WIDTH), BF16),
        scratch_shapes=[pltpu.VMEM((2, N_HEADS, ATT_T), F32), pltpu.VMEM((2, N_HEADS * VT_ROWS, ATT_T), F32),
                        pltpu.VMEM((2, N_HEADS, ATT_T, ATT_T), F32), pltpu.VMEM((2, N_HEADS, ATT_T), F32)],
        compiler_params=pltpu.CompilerParams(dimension_semantics=("parallel", "parallel"),
                                             vmem_limit_bytes=VMEM_LIMIT),
        name="flash",
    )(qt, k, vt)


def _cache_attn_kernel(q_ref, cache_ref, ckr_ref, cnew_ref, krnew_ref, gk_ref, wc_ref, wuk_ref, e_ref,
                       wuv_ref, o_ref, s_ref, *, past, t_new, tk):
    n_tiles = past // tk
    ncol = N_HEADS * t_new
    gk = gk_ref[...]
    qt, qr = [], []
    for hd in range(N_HEADS):
        qg = q_ref[:, hd * LANE:(hd + 1) * LANE].astype(F32) * gk
        qt.append(jnp.dot(qg.astype(BF16), wc_ref[hd], preferred_element_type=F32))
        qr.append(qg[:, :QK_ROPE])
    qt = jnp.concatenate(qt, axis=0).astype(BF16)
    qr = jnp.concatenate(qr, axis=0).astype(BF16)

    nt = (((1,), (1,)), ((), ()))

    def scores_many(tiles):
        cbs = [c.astype(BF16) for c, _ in tiles]
        kns = [jnp.dot(cb, wuk_ref[...], preferred_element_type=F32) for cb in cbs]
        ssqs = [jnp.dot((kn * kn).astype(BF16), e_ref[...], preferred_element_type=F32) for kn in kns]
        invs = [lax.rsqrt((ssq + jnp.sum(kr * kr, axis=-1, keepdims=True)) * (1.0 / QK_HEAD) + EPS)
                for ssq, (_, kr) in zip(ssqs, tiles)]
        ss = [lax.dot_general(cb, qt, nt, preferred_element_type=F32) for cb in cbs]
        ss = [s + lax.dot_general(kr.astype(BF16), qr, nt, preferred_element_type=F32)
              for s, (_, kr) in zip(ss, tiles)]
        return [s * inv for s, inv in zip(ss, invs)]

    def scores(c, kr):
        return scores_many([(c, kr)])[0]

    m = jnp.full((1, ncol), NEG, F32)
    for first in range(0, n_tiles, CACHE_GROUP):
        starts = [j * tk for j in range(first, min(first + CACHE_GROUP, n_tiles))]
        tiles = [(cache_ref[pl.ds(st, tk), :], ckr_ref[:, pl.ds(st, tk)].T) for st in starts]
        for st, s in zip(starts, scores_many(tiles)):
            s_ref[pl.ds(st, tk), :] = s
            m = jnp.maximum(m, jnp.max(s, axis=0, keepdims=True))
    c_new = cnew_ref[...]
    s_new = scores(c_new, krnew_ref[...])
    m = jnp.maximum(m, jnp.max(s_new, axis=0, keepdims=True))

    tn = (((0,), (0,)), ((), ()))

    def accumulate(s, c):
        p = jnp.exp2(s - m)
        ctx = lax.dot_general(p.astype(BF16), c.astype(BF16), tn, preferred_element_type=F32)
        return jnp.sum(p, axis=0, keepdims=True), ctx

    l_old, ctx_old = accumulate(s_ref[...], cache_ref[...])
    l_new, ctx_new = accumulate(s_new, c_new)
    l, ctx = l_old + l_new, ctx_old + ctx_new

    eye = lax.broadcasted_iota(jnp.int32, (ncol, ncol), 0) == lax.broadcasted_iota(jnp.int32, (ncol, ncol), 1)
    l_col = jnp.sum(jnp.where(eye, jnp.broadcast_to(l, (ncol, ncol)), 0.0), axis=1, keepdims=True)
    ctx = (ctx / l_col).astype(BF16)
    r = jnp.dot(ctx, wuv_ref[...], preferred_element_type=F32)
    col_head = lax.broadcasted_iota(jnp.int32, (t_new, ATTN_WIDTH), 1) // V_HEAD
    out = jnp.zeros((t_new, ATTN_WIDTH), F32)
    for hd in range(N_HEADS):
        out = out + jnp.where(col_head == hd, r[hd * t_new:(hd + 1) * t_new, :], 0.0)
    o_ref[...] = out.astype(BF16)


def _cache_attn(q, cache_ckv, cache_krope_t, layer, c_new, kr_new, w, nseq, t_new, tk):
    past = cache_ckv.shape[2]
    assert past % tk == 0
    ncol = N_HEADS * t_new
    return pl.pallas_call(
        functools.partial(_cache_attn_kernel, past=past, t_new=t_new, tk=tk),
        grid=(nseq,),
        in_specs=[
            pl.BlockSpec((t_new, SLAB), lambda b: (b, 0)),
            pl.BlockSpec((None, None, past, KV_LORA), lambda b: (layer, b, 0, 0)),
            pl.BlockSpec((None, None, QK_ROPE, past), lambda b: (layer, b, 0, 0)),
            pl.BlockSpec((t_new, KV_LORA), lambda b: (b, 0)),
            pl.BlockSpec((t_new, QK_ROPE), lambda b: (b, 0)),
        ] + [_const_spec(a.shape) for a in (w["gk1"], w["wc"], w["wuk"], w["e"], w["wv"])],
        out_specs=pl.BlockSpec((t_new, ATTN_WIDTH), lambda b: (b, 0)),
        out_shape=jax.ShapeDtypeStruct((nseq * t_new, ATTN_WIDTH), BF16),
        scratch_shapes=[pltpu.VMEM((past, ncol), F32)],
        compiler_params=pltpu.CompilerParams(dimension_semantics=("parallel",),
                                             vmem_limit_bytes=VMEM_LIMIT),
        name="cache_attn",
    )(q, cache_ckv, cache_krope_t, c_new, kr_new, w["gk1"], w["wc"], w["wuk"], w["e"], w["wv"])


def _merge_kernel(*refs, groups, rows, tiles_per_seq, pos0, has_hist):
    if has_hist:
        x_ref, a_ref, hist_ref = refs[:3]
        refs = refs[3:]
    else:
        x_ref, a_ref = refs[:2]
        hist_ref = None
        refs = refs[2:]
    (gmix_ref, w2_ref, wpool_ref, pscale_ref, wao_ref, wpo_ref, wo_ref, gmlp_ref, wup_ref, wdown_ref,
     y_ref, pool_ref, ext_ref) = refs
    tm = groups * rows
    tile = pl.program_id(0) % tiles_per_seq

    x = x_ref[...]
    h = _rms(x, gmix_ref[...]).astype(BF16)
    z = jnp.dot(h, w2_ref[...], preferred_element_type=F32)
    d = x.shape[1]
    p = z[:, :POOL_WIDTH]
    p3 = p.reshape(groups, rows, POOL_WIDTH)

    group_slices = [slice(g * POOL_GROUP, (g + 1) * POOL_GROUP) for g in range(len(POOL_WINDOWS))]
    for g, sl in enumerate(group_slices):
        if has_hist:
            ext_ref[g, :, :POOL_PAD, :] = hist_ref[:, :, sl]
        else:
            @pl.when(tile == 0)
            def _():
                ext_ref[g, :, :POOL_PAD, :] = jnp.zeros((groups, POOL_PAD, POOL_GROUP), F32)

            @pl.when(tile != 0)
            def _():
                ext_ref[g, :, :POOL_PAD, :] = ext_ref[g, :, rows:rows + POOL_PAD, :]
        ext_ref[g, :, POOL_PAD:, :] = p3[:, :, sl]
    pool_ref[...] = p3[:, rows - POOL_PAD:, :]

    pos = pos0 + tile * rows + lax.broadcasted_iota(jnp.int32, (groups, rows, POOL_GROUP), 1)
    us = []
    for g, win in enumerate(POOL_WINDOWS):
        sl = slice(g * POOL_GROUP, (g + 1) * POOL_GROUP)
        tot = p3[:, :, sl]
        for j in range(1, win):
            tot = tot + ext_ref[g, :, pl.ds(POOL_PAD - j, rows), :]
        cnt = jnp.minimum(win, pos + 1).astype(F32)
        pooled = (tot / cnt - p3[:, :, sl]).reshape(tm, POOL_GROUP)
        u = jnp.dot(pooled.astype(BF16), wpool_ref[g], preferred_element_type=F32)
        us.append((u * pscale_ref[:, sl]).astype(BF16))
    u = jnp.concatenate(us, axis=1)

    branch_a = jnp.dot(a_ref[...], wao_ref[...], preferred_element_type=F32)
    branch_b = jnp.dot(u, wpo_ref[...], preferred_element_type=F32)
    gate_a = 1.0 / (1.0 + jnp.exp(-z[:, POOL_WIDTH:POOL_WIDTH + d]))
    gate_b = 1.0 / (1.0 + jnp.exp(-z[:, POOL_WIDTH + d:]))
    mix = (gate_a * branch_a + gate_b * branch_b).astype(BF16)
    x = x + jnp.dot(mix, wo_ref[...], preferred_element_type=F32)
    hm = _rms(x, gmlp_ref[...]).astype(BF16)
    up = jnp.dot(hm, wup_ref[...], preferred_element_type=F32)
    act = jnp.square(jnp.maximum(up, 0.0)).astype(BF16)
    y_ref[...] = x + jnp.dot(act, wdown_ref[...], preferred_element_type=F32)


def _merge(x, attn, hist, w, nseq, seq, tm, pos0):
    t, d = x.shape
    assert t % tm == 0
    if hist is None:
        assert seq % tm == 0
        groups, rows, tiles_per_seq = 1, tm, seq // tm
    else:
        assert tm % seq == 0 and seq >= POOL_PAD
        groups, rows, tiles_per_seq = tm // seq, seq, 1
    assert rows % 8 == 0 and rows >= POOL_PAD
    row = lambda i: (i, 0)
    seq_block = lambda i: (i // tiles_per_seq, 0, 0)
    in_specs = [pl.BlockSpec((tm, d), row), pl.BlockSpec((tm, ATTN_WIDTH), row)]
    args = [x, attn]
    if hist is not None:
        in_specs.append(pl.BlockSpec((groups, POOL_PAD, POOL_WIDTH), seq_block))
        args.append(hist)
    weights = (w["g_mix"], w["w2"], w["w_pool"], w["pool_scale"], w["w_attn_out"], w["w_pool_out"],
               w["w_o"], w["g_mlp"], w["w_up"], w["w_down"])
    in_specs += [_const_spec(a.shape) for a in weights]
    return pl.pallas_call(
        functools.partial(_merge_kernel, groups=groups, rows=rows, tiles_per_seq=tiles_per_seq,
                          pos0=pos0, has_hist=hist is not None),
        grid=(t // tm,),
        in_specs=in_specs,
        out_specs=(pl.BlockSpec((tm, d), row),
                   pl.BlockSpec((groups, POOL_PAD, POOL_WIDTH), seq_block)),
        out_shape=(jax.ShapeDtypeStruct((t, d), F32),
                   jax.ShapeDtypeStruct((nseq, POOL_PAD, POOL_WIDTH), F32)),
        scratch_shapes=[pltpu.VMEM((len(POOL_WINDOWS), groups, POOL_PAD + rows, POOL_GROUP), F32)],
        compiler_params=pltpu.CompilerParams(dimension_semantics=("arbitrary",),
                                             vmem_limit_bytes=VMEM_LIMIT),
        name="merge",
    )(*args, *weights)


def _rope_tables(pos):
    half = QK_ROPE // 2
    inv = jnp.power(ROPE_THETA, -jnp.arange(half, dtype=F32) / half)
    ang = pos[:, None] * inv[None, :]
    cos, sin = jnp.cos(ang), jnp.sin(ang)
    n = pos.shape[0]
    one = jnp.ones((n, LANE - QK_ROPE), F32)
    zero = jnp.zeros((n, LANE - half), F32)
    c = jnp.concatenate([cos, cos, one], axis=1)
    s1 = jnp.concatenate([jnp.zeros((n, half), F32), sin, jnp.zeros((n, LANE - QK_ROPE), F32)], axis=1)
    s2 = jnp.concatenate([-sin, zero], axis=1)
    return c, s1, s2, cos.T, sin.T


def _slab(rope_part, nope_part):
    pad = jnp.zeros(rope_part.shape[:-1] + (LANE - QK_HEAD,), rope_part.dtype)
    s = jnp.concatenate([rope_part, nope_part, pad], axis=-1)
    return s.reshape(s.shape[:-2] + (N_HEADS * LANE,))


def _layer_weights(l, t_new, g_mix, w_in, g_qa, g_kva, w_uq, w_ukv, g_q, g_k, w_attn_out, w_pool,
                   pool_scale, w_pool_out, w_o, g_mlp, w_up, w_down):
    off_kv, off_kr = Q_LORA, Q_LORA + KV_LORA
    off_p = off_kr + QK_ROPE
    wi = w_in[l]
    d = wi.shape[0]
    w1 = jnp.concatenate([wi[:, :off_p], jnp.zeros((d, LANE - QK_ROPE), F32)], axis=1)
    uq = w_uq[l]
    ukv = w_ukv[l]
    k_nope, v = ukv[..., :QK_NOPE], ukv[..., QK_NOPE:]
    zr = jnp.zeros(k_nope.shape[:-1] + (QK_ROPE,), F32)
    gq, gk = g_q[l], g_k[l]
    slab_gain = lambda g: jnp.concatenate([g[QK_NOPE:], g[:QK_NOPE], jnp.zeros((LANE - QK_HEAD,), F32)])[None]
    wc = jnp.zeros((N_HEADS, LANE, KV_LORA), F32).at[:, QK_ROPE:QK_HEAD, :].set(k_nope.transpose(1, 2, 0))
    ncol = N_HEADS * t_new
    e = (jnp.arange(N_HEADS * QK_NOPE)[:, None] // QK_NOPE == jnp.arange(ncol)[None, :] // t_new)
    return {
        "g_mix": g_mix[l][None],
        "w1": w1.astype(BF16),
        "w2": wi[:, off_p:].astype(BF16),
        "g_qa": g_qa[l][None],
        "g_kva": g_kva[l][None],
        "wq": _slab(uq[..., QK_NOPE:], uq[..., :QK_NOPE]).astype(BF16),
        "wq_t": _slab(uq[..., QK_NOPE:], uq[..., :QK_NOPE]).T.astype(BF16),
        "wk": _slab(zr, k_nope).astype(BF16),
        "wv": v.reshape(KV_LORA, ATTN_WIDTH).astype(BF16),
        "wv_t": v.reshape(KV_LORA, ATTN_WIDTH).T.astype(BF16),
        "wuk": k_nope.reshape(KV_LORA, N_HEADS * QK_NOPE).astype(BF16),
        "wc": wc.astype(BF16),
        "e": e.astype(BF16),
        "gq": slab_gain(gq) * (SM_SCALE * LOG2E),
        "gq_t": slab_gain(gq).T * (SM_SCALE * LOG2E),
        "gk": slab_gain(gk),
        "gk1": slab_gain(gk),
        "w_attn_out": w_attn_out[l].astype(BF16),
        "w_pool": w_pool[l].astype(BF16),
        "pool_scale": pool_scale[l][None],
        "w_pool_out": w_pool_out[l].astype(BF16),
        "w_o": w_o[l].astype(BF16),
        "g_mlp": g_mlp[l][None],
        "w_up": w_up[l].astype(BF16),
        "w_down": w_down[l].astype(BF16),
    }


def kernel(x_prompt, x_sample, cache_ckv, cache_krope, state_pool, g_mix, w_in, g_qa, g_kva, w_uq, w_ukv,
           g_q, g_k, w_attn_out, w_pool, pool_scale, w_pool_out, w_o, g_mlp, w_up, w_down):
    batch, seq, d = x_prompt.shape
    nseq, t_new, _ = x_sample.shape
    depth = g_mix.shape[0]
    past = cache_ckv.shape[2]

    tabs_p = _rope_tables(jnp.arange(seq, dtype=F32))
    proj_tm_s = min(PROJ_TM, nseq * t_new)
    reps = proj_tm_s // t_new
    tabs_s = _rope_tables(past + jnp.arange(t_new, dtype=F32))
    tabs_s = tuple(jnp.tile(a, (reps, 1)) for a in tabs_s[:3]) + tuple(jnp.tile(a, (1, reps)) for a in tabs_s[3:])
    hist = jnp.pad(state_pool, ((0, 0), (0, 0), (POOL_PAD - POOL_HIST, 0), (0, 0)))
    krope_t = jnp.swapaxes(cache_krope, 2, 3)

    xp = x_prompt.reshape(batch * seq, d)
    xs = x_sample.reshape(nseq * t_new, d)
    outs = {k: [] for k in ("kr_p", "pool_p", "kr_s", "pool_s")}
    ckv_p = ckv_s = None
    for l in range(depth):
        w = _layer_weights(l, t_new, g_mix, w_in, g_qa, g_kva, w_uq, w_ukv, g_q, g_k, w_attn_out, w_pool,
                           pool_scale, w_pool_out, w_o, g_mlp, w_up, w_down)
        qt, k, vt, ckv_p, kr = _proj(xp, tabs_p, seq // PROJ_TM, w, PROJ_TM, True, l, depth, ckv_p)
        attn = _flash(qt, k, vt, batch, seq)
        xp, pool = _merge(xp, attn, None, w, batch, seq, MERGE_TM, 0)
        outs["kr_p"].append(kr.reshape(batch, seq, QK_ROPE))
        outs["pool_p"].append(pool[:, POOL_PAD - POOL_HIST:])

        q, ckv_s, kr = _proj(xs, tabs_s, 1, w, proj_tm_s, False, l, depth, ckv_s)
        attn = _cache_attn(q, cache_ckv, krope_t, l, ckv_s[l], kr, w, nseq, t_new, CACHE_TK)
        xs, pool = _merge(xs, attn, hist[l], w, nseq, t_new, min(MERGE_TM, nseq * t_new), past)
        outs["kr_s"].append(kr.reshape(nseq, t_new, QK_ROPE))
        outs["pool_s"].append(pool[:, POOL_PAD - POOL_HIST:])

    return (xp.reshape(batch, seq, d), xs.reshape(nseq, t_new, d),
            ckv_p.reshape(depth, batch, seq, KV_LORA), jnp.stack(outs["kr_p"]), jnp.stack(outs["pool_p"]),
            ckv_s.reshape(depth, nseq, t_new, KV_LORA), jnp.stack(outs["kr_s"]), jnp.stack(outs["pool_s"]))
```
